```python
import jax, jax.numpy as jnp
from jax import lax
import numpy as np

D_MODEL = 2048
BATCH = 4
SEQ = 4096
DEPTH = 2

CHUNK = 64

D_A = 768
K_A = 31
POOL_WINDOWS = (2, 4, 8, 16)
N_POOL_GROUPS = 4
POOL_GROUP = 128
D_B = N_POOL_GROUPS * POOL_GROUP
D_C = 768
K_C = 3
D_IN = 2 * D_A + D_B + 3 * D_C
N_BRANCH = 3
D_FF = 4 * D_MODEL
EPS = 1e-6

kernel_name = "hybrid_conformer_pool_shortconv_gated"


def rmsnorm(x, g):
    xf = x.astype(jnp.float32)
    y = xf * lax.rsqrt(jnp.mean(xf * xf, axis=-1, keepdims=True) + EPS)
    return (y * g.astype(jnp.float32)).astype(x.dtype)


def layernorm(x, g, b):
    xf = x.astype(jnp.float32)
    mu = jnp.mean(xf, axis=-1, keepdims=True)
    xc = xf - mu
    var = jnp.mean(xc * xc, axis=-1, keepdims=True)
    y = xc * lax.rsqrt(var + EPS) * g.astype(jnp.float32) + b.astype(jnp.float32)
    return y.astype(x.dtype)


def causal_dwconv(u, w):
    k, c = w.shape
    return lax.conv_general_dilated(
        u, w[:, None, :].astype(u.dtype), window_strides=(1,), padding=[(k - 1, 0)],
        dimension_numbers=("NWC", "WIO", "NWC"), feature_group_count=c)


def multiscale_pool(u, w_grp, scale):
    b, s, _ = u.shape
    ug = u.reshape(b, s, N_POOL_GROUPS, POOL_GROUP).astype(jnp.float32)
    cs = jnp.cumsum(ug, axis=1)
    pos = jnp.arange(1, s + 1, dtype=jnp.float32)
    means = []
    for gi, w in enumerate(POOL_WINDOWS):
        c = cs[:, :, gi]
        lag = jnp.pad(c, ((0, 0), (w, 0), (0, 0)))[:, :s]
        cnt = jnp.minimum(pos, float(w))[None, :, None]
        means.append((c - lag) / cnt)
    pooled = (jnp.stack(means, axis=2) - ug).astype(u.dtype)
    mixed = jnp.einsum("bsgc,gcd->bsgd", pooled, w_grp)
    return mixed.reshape(b, s, D_B) * scale


def setup_inputs(seed: int = 0) -> dict:
    key = jax.random.key(seed)
    ks = jax.random.split(key, 24)
    f32 = jnp.float32
    nrm = lambda k, shape, scale: jax.random.normal(k, shape, f32) * scale
    L, D = DEPTH, D_MODEL
    return {
        "x": jax.random.normal(ks[0], (BATCH, SEQ, D), f32),
        "g_mix": 1.0 + nrm(ks[1], (L, D), 0.02),
        "w_in": nrm(ks[2], (L, D, D_IN), D ** -0.5),
        "w_gate": nrm(ks[3], (L, D, N_BRANCH * D), D ** -0.5),
        "b_gate": nrm(ks[4], (L, N_BRANCH * D), 0.01),
        "b_glu": nrm(ks[5], (L, 2 * D_A), 0.01),
        "w_dw_a": nrm(ks[6], (L, K_A, D_A), K_A ** -0.5),
        "b_dw_a": nrm(ks[7], (L, D_A), 0.01),
        "ln_g_a": 1.0 + nrm(ks[8], (L, D_A), 0.02),
        "ln_b_a": nrm(ks[9], (L, D_A), 0.01),
        "w_a_out": nrm(ks[10], (L, D_A, D), D_A ** -0.5),
        "w_pool_grp": nrm(ks[11], (L, N_POOL_GROUPS, POOL_GROUP, POOL_GROUP), POOL_GROUP ** -0.5),
        "pool_scale": 1.0 + nrm(ks[12], (L, D_B), 0.02),
        "w_b_out": nrm(ks[13], (L, D_B, D), D_B ** -0.5),
        "w_dw_c": nrm(ks[14], (L, K_C, D_C), K_C ** -0.5),
        "w_c_out": nrm(ks[15], (L, D_C, D), D_C ** -0.5),
        "w_o": nrm(ks[16], (L, D, D), D ** -0.5),
        "g_mlp": 1.0 + nrm(ks[17], (L, D), 0.02),
        "w_up": nrm(ks[18], (L, D, D_FF), D ** -0.5),
        "w_down": nrm(ks[19], (L, D_FF, D), 0.5 * D_FF ** -0.5),
        "g_final": 1.0 + nrm(ks[20], (D,), 0.02),
    }


def reference(x, g_mix, w_in, w_gate, b_gate, b_glu, w_dw_a, b_dw_a, ln_g_a, ln_b_a, w_a_out,
              w_pool_grp, pool_scale, w_b_out, w_dw_c, w_c_out, w_o, g_mlp, w_up, w_down, g_final):
    b, s, d = x.shape
    for l in range(DEPTH):
        h = rmsnorm(x, g_mix[l])
        z = h @ w_in[l]
        za = z[..., : 2 * D_A] + b_glu[l]
        zb = z[..., 2 * D_A: 2 * D_A + D_B]
        zc = z[..., 2 * D_A + D_B:]

        ua = za[..., :D_A] * jax.nn.sigmoid(za[..., D_A:])
        va = causal_dwconv(ua, w_dw_a[l]) + b_dw_a[l]
        va = jax.nn.silu(layernorm(va, ln_g_a[l], ln_b_a[l]))
        ya = va @ w_a_out[l]

        yb = multiscale_pool(zb, w_pool_grp[l], pool_scale[l]) @ w_b_out[l]

        gb, gc, xv = zc[..., :D_C], zc[..., D_C: 2 * D_C], zc[..., 2 * D_C:]
        yc = (gb * causal_dwconv(gc * xv, w_dw_c[l])) @ w_c_out[l]

        gates = jax.nn.sigmoid(h @ w_gate[l] + b_gate[l]).reshape(b, s, N_BRANCH, d)
        merged = gates[:, :, 0] * ya + gates[:, :, 1] * yb + gates[:, :, 2] * yc
        x = x + merged @ w_o[l]

        h2 = rmsnorm(x, g_mlp[l])
        x = x + jnp.square(jax.nn.relu(h2 @ w_up[l])) @ w_down[l]
    return rmsnorm(x, g_final)
```

```python
import functools

import jax
import jax.numpy as jnp
from jax import lax
from jax.experimental import pallas as pl
from jax.experimental.pallas import tpu as pltpu

D_MODEL = 2048
DEPTH = 2
D_A = 768
K_A = 31
POOL_WINDOWS = (2, 4, 8, 16)
POOL_GROUP = 128
D_B = len(POOL_WINDOWS) * POOL_GROUP
D_C = 768
K_C = 3
D_IN = 2 * D_A + D_B + 3 * D_C
N_BRANCH = 3
D_FF = 4 * D_MODEL
EPS = 1e-6

OFF_B = 2 * D_A
OFF_C = OFF_B + D_B

HALO_A = 32
HALO_B = 16
HALO_C = 8

V7X_VMEM_LIMIT = 60 * 1024 * 1024

BF16 = jnp.bfloat16
F32 = jnp.float32


def _rmsnorm(x, g):
    return x * lax.rsqrt(jnp.mean(x * x, axis=-1, keepdims=True) + EPS) * g


def _norm_proj_kernel(x_ref, g_ref, w_ref, b_ref, o_ref, *, act):
    h = _rmsnorm(x_ref[...], g_ref[...]).astype(BF16)
    y = jnp.dot(h, w_ref[...], preferred_element_type=F32) + b_ref[...]
    if act == "sigmoid":
        y = jax.nn.sigmoid(y)
    o_ref[...] = y.astype(o_ref.dtype)


def _norm_proj(x, g, w, b, *, act, tm, tn):
    m, d = x.shape
    n = w.shape[1]
    return pl.pallas_call(
        functools.partial(_norm_proj_kernel, act=act),
        grid=(n // tn, m // tm),
        in_specs=[
            pl.BlockSpec((tm, d), lambda j, i: (i, 0)),
            pl.BlockSpec((1, d), lambda j, i: (0, 0)),
            pl.BlockSpec((d, tn), lambda j, i: (0, j)),
            pl.BlockSpec((1, tn), lambda j, i: (0, j)),
        ],
        out_specs=pl.BlockSpec((tm, tn), lambda j, i: (i, j)),
        out_shape=jax.ShapeDtypeStruct((m, n), BF16),
        compiler_params=pltpu.CompilerParams(
            dimension_semantics=("arbitrary", "arbitrary"), vmem_limit_bytes=V7X_VMEM_LIMIT),
        name="norm_proj_" + act,
    )(x, g, w, b)


CONV_ROWS = 32


def _mix_kernel(z_ref, gates_ref, x_ref, wdwa_ref, bdwa_ref, lng_ref, lnb_ref, wpool_ref, pscale_ref, wdwc_ref,
                wa_ref, wb_ref, wc_ref, wo_ref, o_ref, ua_ext, zb_ext, p_ext, va_ref, *, ts):
    s = pl.program_id(1)

    @pl.when(s == 0)
    def _():
        ua_ext[0:HALO_A, :] = jnp.zeros((HALO_A, D_A), F32)
        zb_ext[0:HALO_B, :] = jnp.zeros((HALO_B, D_B), F32)
        p_ext[0:HALO_C, :] = jnp.zeros((HALO_C, D_C), F32)

    ua_ext[HALO_A:HALO_A + ts, :] = (
        z_ref[:, 0:D_A].astype(F32) * jax.nn.sigmoid(z_ref[:, D_A:2 * D_A].astype(F32)))

    for base in range(0, ts, CONV_ROWS):
        acc = jnp.zeros((CONV_ROWS, D_A), F32)
        for lag in range(K_A):
            tap = K_A - 1 - lag
            lo = base + HALO_A - lag
            acc = acc + ua_ext[lo:lo + CONV_ROWS, :] * wdwa_ref[tap:tap + 1, :]
        acc = acc + bdwa_ref[...]
        mu = jnp.mean(acc, axis=-1, keepdims=True)
        xc = acc - mu
        var = jnp.mean(xc * xc, axis=-1, keepdims=True)
        y = xc * lax.rsqrt(var + EPS) * lng_ref[...] + lnb_ref[...]
        y = y * jax.nn.sigmoid(y)
        va_ref[base:base + CONV_ROWS, :] = y.astype(BF16)
    ya = jnp.dot(va_ref[...], wa_ref[...], preferred_element_type=F32)

    zb = z_ref[:, OFF_B:OFF_B + D_B].astype(F32)
    zb_ext[HALO_B:HALO_B + ts, :] = zb
    seen = (s * ts + 1 + lax.broadcasted_iota(jnp.int32, (ts, 1), 0)).astype(F32)
    mixed = []
    for gi, w in enumerate(POOL_WINDOWS):
        lo = gi * POOL_GROUP
        win = zb[:, lo:lo + POOL_GROUP]
        for lag in range(1, w):
            win = win + zb_ext[HALO_B - lag:HALO_B - lag + ts, lo:lo + POOL_GROUP]
        pooled = win / jnp.minimum(seen, float(w)) - zb[:, lo:lo + POOL_GROUP]
        mixed.append(jnp.dot(pooled.astype(BF16), wpool_ref[gi], preferred_element_type=F32))
    mixed = jnp.concatenate(mixed, axis=-1) * pscale_ref[...]
    yb = jnp.dot(mixed.astype(BF16), wb_ref[...], preferred_element_type=F32)

    p = z_ref[:, OFF_C + D_C:OFF_C + 2 * D_C].astype(F32) * z_ref[:, OFF_C + 2 * D_C:OFF_C + 3 * D_C].astype(F32)
    p_ext[HALO_C:HALO_C + ts, :] = p
    cv = p * wdwc_ref[K_C - 1:K_C, :]
    for lag in range(1, K_C):
        tap = K_C - 1 - lag
        cv = cv + p_ext[HALO_C - lag:HALO_C - lag + ts, :] * wdwc_ref[tap:tap + 1, :]
    gb = z_ref[:, OFF_C:OFF_C + D_C].astype(F32)
    yc = jnp.dot((gb * cv).astype(BF16), wc_ref[...], preferred_element_type=F32)

    merged = (gates_ref[:, 0:D_MODEL].astype(F32) * ya
              + gates_ref[:, D_MODEL:2 * D_MODEL].astype(F32) * yb
              + gates_ref[:, 2 * D_MODEL:3 * D_MODEL].astype(F32) * yc)
    o_ref[...] = x_ref[...] + jnp.dot(merged.astype(BF16), wo_ref[...], preferred_element_type=F32)

    ua_ext[0:HALO_A, :] = ua_ext[ts:ts + HALO_A, :]
    zb_ext[0:HALO_B, :] = zb_ext[ts:ts + HALO_B, :]
    p_ext[0:HALO_C, :] = p_ext[ts:ts + HALO_C, :]


def _mix(z, gates, x, wdwa, bdwa, lng, lnb, wpool, pscale, wdwc, wa, wb, wc, wo, *, batch, ts):
    m, d = x.shape
    n_s = m // batch // ts
    row = lambda b, s: (b * n_s + s, 0)
    const2 = lambda b, s: (0, 0)
    const3 = lambda b, s: (0, 0, 0)
    resident = functools.partial(pl.BlockSpec, pipeline_mode=pl.Buffered(1))
    return pl.pallas_call(
        functools.partial(_mix_kernel, ts=ts),
        grid=(batch, n_s),
        in_specs=[
            pl.BlockSpec((ts, D_IN), row),
            pl.BlockSpec((ts, N_BRANCH * d), row),
            pl.BlockSpec((ts, d), row),
            resident((K_A, D_A), const2),
            resident((1, D_A), const2),
            resident((1, D_A), const2),
            resident((1, D_A), const2),
            resident((len(POOL_WINDOWS), POOL_GROUP, POOL_GROUP), const3),
            resident((1, D_B), const2),
            resident((K_C, D_C), const2),
            resident((D_A, d), const2),
            resident((D_B, d), const2),
            resident((D_C, d), const2),
            resident((d, d), const2),
        ],
        out_specs=pl.BlockSpec((ts, d), row),
        out_shape=jax.ShapeDtypeStruct((m, d), F32),
        scratch_shapes=[
            pltpu.VMEM((HALO_A + ts, D_A), F32),
            pltpu.VMEM((HALO_B + ts, D_B), F32),
            pltpu.VMEM((HALO_C + ts, D_C), F32),
            pltpu.VMEM((ts, D_A), BF16),
        ],
        compiler_params=pltpu.CompilerParams(
            dimension_semantics=("arbitrary", "arbitrary"), vmem_limit_bytes=V7X_VMEM_LIMIT),
        name="mix",
    )(z, gates, x, wdwa, bdwa, lng, lnb, wpool, pscale, wdwc, wa, wb, wc, wo)


def _mlp_kernel(x_ref, g_ref, wu_ref, wd_ref, gf_ref, o_ref, h_ref, *, final):
    f = pl.program_id(1)

    @pl.when(f == 0)
    def _():
        x = x_ref[...]
        h_ref[...] = _rmsnorm(x, g_ref[...]).astype(BF16)
        o_ref[...] = x

    a = jnp.dot(h_ref[...], wu_ref[...], preferred_element_type=F32)
    a = jnp.square(jnp.maximum(a, 0.0)).astype(BF16)
    o_ref[...] += jnp.dot(a, wd_ref[...], preferred_element_type=F32)

    if final:
        @pl.when(f == pl.num_programs(1) - 1)
        def _():
            o_ref[...] = _rmsnorm(o_ref[...], gf_ref[...])


def _mlp(x, g, wu, wd, gf, *, final, tm, tf):
    m, d = x.shape
    ff = wu.shape[1]
    return pl.pallas_call(
        functools.partial(_mlp_kernel, final=final),
        grid=(m // tm, ff // tf),
        in_specs=[
            pl.BlockSpec((tm, d), lambda i, f: (i, 0)),
            pl.BlockSpec((1, d), lambda i, f: (0, 0)),
            pl.BlockSpec((d, tf), lambda i, f: (0, f)),
            pl.BlockSpec((tf, d), lambda i, f: (f, 0)),
            pl.BlockSpec((1, d), lambda i, f: (0, 0)),
        ],
        out_specs=pl.BlockSpec((tm, d), lambda i, f: (i, 0)),
        out_shape=jax.ShapeDtypeStruct((m, d), F32),
        scratch_shapes=[pltpu.VMEM((tm, d), BF16)],
        compiler_params=pltpu.CompilerParams(
            dimension_semantics=("arbitrary", "arbitrary"), vmem_limit_bytes=V7X_VMEM_LIMIT),
        name="mlp_final" if final else "mlp",
    )(x, g, wu, wd, gf)


def kernel(x, g_mix, w_in, w_gate, b_gate, b_glu, w_dw_a, b_dw_a, ln_g_a, ln_b_a, w_a_out, w_pool_grp, pool_scale,
           w_b_out, w_dw_c, w_c_out, w_o, g_mlp, w_up, w_down, g_final):
    b, s, d = x.shape
    assert (d, w_in.shape[-1], w_up.shape[-1]) == (D_MODEL, D_IN, D_FF)
    xf = x.reshape(b * s, d)
    row = lambda v: v.reshape(1, -1)
    gf = row(g_final)
    for l in range(DEPTH):
        b_in = jnp.concatenate([b_glu[l], jnp.zeros((D_IN - 2 * D_A,), F32)]).reshape(1, D_IN)
        z = _norm_proj(xf, row(g_mix[l]), w_in[l].astype(BF16), b_in, act="none", tm=512, tn=D_IN // 2)
        gates = _norm_proj(xf, row(g_mix[l]), w_gate[l].astype(BF16), row(b_gate[l]), act="sigmoid", tm=512, tn=2048)
        xf = _mix(z, gates, xf, w_dw_a[l], row(b_dw_a[l]), row(ln_g_a[l]), row(ln_b_a[l]),
                  w_pool_grp[l].astype(BF16), row(pool_scale[l]), w_dw_c[l],
                  w_a_out[l].astype(BF16), w_b_out[l].astype(BF16), w_c_out[l].astype(BF16), w_o[l].astype(BF16),
                  batch=b, ts=256)
        xf = _mlp(xf, row(g_mlp[l]), w_up[l].astype(BF16), w_down[l].astype(BF16), gf,
                  final=(l == DEPTH - 1), tm=1024, tf=512)
    return xf.reshape(b, s, d)
```

```python
import functools

import jax
import jax.numpy as jnp
from jax import lax
from jax.experimental import pallas as pl
from jax.experimental.pallas import tpu as pltpu

D_MODEL = 2048
DEPTH = 2
D_A = 768
K_A = 31
POOL_WINDOWS = (2, 4, 8, 16)
POOL_GROUP = 128
D_B = len(POOL_WINDOWS) * POOL_GROUP
D_C = 768
K_C = 3
D_IN = 2 * D_A + D_B + 3 * D_C
N_BRANCH = 3
D_FF = 4 * D_MODEL
EPS = 1e-6

OFF_AG = D_A
OFF_B = 2 * D_A
OFF_GB = OFF_B + D_B
OFF_GC = OFF_GB + D_C
OFF_XV = OFF_GC + D_C
D_ACTS = D_A + D_B + D_C
ACT_B = D_A
ACT_C = D_A + D_B

SUBLANES = 8
LANES = 128
HALO_A = 32
HALO_B = 16
HALO_C = 8
CONV_GROUPS = 8
Z_BLOCK = 256
Z_CHUNK = 2 * Z_BLOCK

V7X_VMEM_LIMIT = 60 * 1024 * 1024

BF16 = jnp.bfloat16
F32 = jnp.float32


def _rmsnorm(x, g):
    return x * lax.rsqrt(jnp.mean(x * x, axis=-1, keepdims=True) + EPS) * g


def _params(*sem, flags=None):
    return pltpu.CompilerParams(dimension_semantics=sem, vmem_limit_bytes=V7X_VMEM_LIMIT, flags=flags)


def _in_mix_kernel(x_ref, g_ref, win_ref, bin_ref, wdwa_ref, bdwa_ref, lng_ref, lnb_ref, wpool_ref, pscale_ref,
                   wdwc_ref, h_ref, acts_ref, ua_next, zb_next, gb_next, p_next, gb_cur, ua_ext, cv_buf, zb_ext, p_ext,
                   *, tm, tiles_per_seq):
    t = pl.program_id(0)
    s_prev = lax.rem(t + tiles_per_seq - 1, tiles_per_seq)

    @pl.when(t == 0)
    def _():
        ua_next[...] = jnp.zeros_like(ua_next)
        zb_next[...] = jnp.zeros_like(zb_next)
        gb_next[...] = jnp.zeros_like(gb_next)
        p_next[...] = jnp.zeros_like(p_next)

    @pl.when(jnp.logical_or(t == 0, s_prev == 0))
    def _():
        ua_ext[0:HALO_A, :] = jnp.zeros((HALO_A, D_A), F32)
        zb_ext[0:HALO_B, :] = jnp.zeros((HALO_B, D_B), F32)
        p_ext[0:HALO_C, :] = jnp.zeros((HALO_C, D_C), F32)

    ua_ext[HALO_A:HALO_A + tm, :] = ua_next[...]
    zb_ext[HALO_B:HALO_B + tm, :] = zb_next[...]
    p_ext[HALO_C:HALO_C + tm, :] = p_next[...]
    gb_cur[...] = gb_next[...]

    h_ref[...] = _rmsnorm(x_ref[...], g_ref[...]).astype(BF16)

    def put_block(lo, zc):
        if lo < OFF_AG:
            ua_next[:, lo:lo + Z_BLOCK] = zc
        elif lo < OFF_B:
            c = lo - OFF_AG
            ua_next[:, c:c + Z_BLOCK] = ua_next[:, c:c + Z_BLOCK] * jax.nn.sigmoid(zc)
        elif lo < OFF_GB:
            zb_next[:, lo - OFF_B:lo - OFF_B + Z_BLOCK] = zc
        elif lo < OFF_GC:
            gb_next[:, lo - OFF_GB:lo - OFF_GB + Z_BLOCK] = zc
        elif lo < OFF_XV:
            p_next[:, lo - OFF_GC:lo - OFF_GC + Z_BLOCK] = zc
        else:
            c = lo - OFF_XV
            p_next[:, c:c + Z_BLOCK] = p_next[:, c:c + Z_BLOCK] * zc

    def project(lo):
        hi = min(lo + Z_CHUNK, D_IN)
        zc = jnp.dot(h_ref[...], win_ref[:, lo:hi], preferred_element_type=F32) + bin_ref[:, lo:hi]
        for b in range(0, hi - lo, Z_BLOCK):
            put_block(lo + b, zc[:, b:b + Z_BLOCK])

    rows = CONV_GROUPS * SUBLANES
    row_in_vreg = lax.broadcasted_iota(jnp.int32, (SUBLANES, LANES), 0)

    def conv_block(base, c):
        lanes = slice(c * LANES, (c + 1) * LANES)
        lo = base
        n_win = HALO_A // SUBLANES + CONV_GROUPS
        win = [ua_ext[lo + SUBLANES * j:lo + SUBLANES * (j + 1), lanes] for j in range(n_win)]
        acc = [jnp.zeros((SUBLANES, LANES), F32) for _ in range(CONV_GROUPS)]
        for r in range(SUBLANES):
            if r == 0:
                moved = win
            else:
                rolled = [pltpu.roll(v, r, 0) for v in win]
                moved = [None] + [jnp.where(row_in_vreg >= r, rolled[j], rolled[j - 1]) for j in range(1, n_win)]
            for q in range(HALO_A // SUBLANES):
                lag = SUBLANES * q + r
                if lag >= K_A:
                    continue
                wt = wdwa_ref[K_A - 1 - lag, :, lanes]
                for gi in range(CONV_GROUPS):
                    acc[gi] = acc[gi] + moved[HALO_A // SUBLANES + gi - q] * wt
        cv_buf[base:base + rows, lanes] = jnp.concatenate(acc, axis=0)

    def norm_rows(base):
        va = cv_buf[base:base + rows, :] + bdwa_ref[...]
        mu = jnp.mean(va, axis=-1, keepdims=True)
        xc = va - mu
        var = jnp.mean(xc * xc, axis=-1, keepdims=True)
        y = xc * lax.rsqrt(var + EPS) * lng_ref[...] + lnb_ref[...]
        y = y * jax.nn.sigmoid(y)
        acts_ref[base:base + rows, 0:D_A] = y.astype(BF16)

    def pool_group(gi):
        w = POOL_WINDOWS[gi]
        lo = gi * POOL_GROUP
        seen = (s_prev * tm + 1 + lax.broadcasted_iota(jnp.int32, (tm, 1), 0)).astype(F32)
        tok = zb_ext[HALO_B:HALO_B + tm, lo:lo + POOL_GROUP]
        win = tok
        for lag in range(1, w):
            win = win + zb_ext[HALO_B - lag:HALO_B - lag + tm, lo:lo + POOL_GROUP]
        pooled = win / jnp.minimum(seen, float(w)) - tok
        mixed = jnp.dot(pooled.astype(BF16), wpool_ref[gi], preferred_element_type=F32)
        acts_ref[:, ACT_B + lo:ACT_B + lo + POOL_GROUP] = (mixed * pscale_ref[:, lo:lo + POOL_GROUP]).astype(BF16)

    def short_conv():
        cv = p_ext[HALO_C:HALO_C + tm, :] * wdwc_ref[K_C - 1:K_C, :]
        for lag in range(1, K_C):
            tap = K_C - 1 - lag
            cv = cv + p_ext[HALO_C - lag:HALO_C - lag + tm, :] * wdwc_ref[tap:tap + 1, :]
        acts_ref[:, ACT_C:ACT_C + D_C] = (gb_cur[...] * cv).astype(BF16)

    def keep_tails():
        ua_ext[0:HALO_A, :] = ua_ext[tm:tm + HALO_A, :]
        zb_ext[0:HALO_B, :] = zb_ext[tm:tm + HALO_B, :]
        p_ext[0:HALO_C, :] = p_ext[tm:tm + HALO_C, :]

    mixer_work = []
    for base in range(0, tm, rows):
        mixer_work += [functools.partial(conv_block, base, c) for c in range(D_A // LANES)]
        mixer_work += [functools.partial(norm_rows, base)]
    mixer_work += ([functools.partial(pool_group, gi) for gi in range(len(POOL_WINDOWS))]
                  + [short_conv, keep_tails])
    chunks = list(range(0, D_IN, Z_CHUNK))
    per_chunk = -(-len(mixer_work) // (len(chunks) - 1))
    for ci, lo in enumerate(chunks):
        project(lo)
        for work in mixer_work[ci * per_chunk:(ci + 1) * per_chunk]:
            work()


def _in_mix(x, g, win, b_in, wdwa8, bdwa, lng, lnb, wpool, pscale, wdwc, *, seq, tm):
    m, d = x.shape
    n_t = m // tm
    cur = lambda t: (jnp.minimum(t, n_t - 1), 0)
    prev = lambda t: (jnp.maximum(t - 1, 0), 0)
    const2 = lambda t: (0, 0)
    const3 = lambda t: (0, 0, 0)
    resident = functools.partial(pl.BlockSpec, pipeline_mode=pl.Buffered(1))
    return pl.pallas_call(
        functools.partial(_in_mix_kernel, tm=tm, tiles_per_seq=seq // tm),
        grid=(n_t + 1,),
        in_specs=[
            pl.BlockSpec((tm, d), cur),
            resident((1, d), const2),
            resident((d, D_IN), const2),
            resident((1, D_IN), const2),
            resident((K_A, SUBLANES, D_A), const3),
            resident((1, D_A), const2),
            resident((1, D_A), const2),
            resident((1, D_A), const2),
            resident((len(POOL_WINDOWS), POOL_GROUP, POOL_GROUP), const3),
            resident((1, D_B), const2),
            resident((K_C, D_C), const2),
        ],
        out_specs=[pl.BlockSpec((tm, d), cur), pl.BlockSpec((tm, D_ACTS), prev)],
        out_shape=[jax.ShapeDtypeStruct((m, d), BF16), jax.ShapeDtypeStruct((m, D_ACTS), BF16)],
        scratch_shapes=[
            pltpu.VMEM((tm, D_A), F32),
            pltpu.VMEM((tm, D_B), F32),
            pltpu.VMEM((tm, D_C), F32),
            pltpu.VMEM((tm, D_C), F32),
            pltpu.VMEM((tm, D_C), F32),
            pltpu.VMEM((HALO_A + tm, D_A), F32),
            pltpu.VMEM((tm, D_A), F32),
            pltpu.VMEM((HALO_B + tm, D_B), F32),
            pltpu.VMEM((HALO_C + tm, D_C), F32),
        ],
        compiler_params=_params("arbitrary"),
        name="in_mix",
    )(x, g, win, b_in, wdwa8, bdwa, lng, lnb, wpool, pscale, wdwc)


def _gate_merge_kernel(h_ref, acts_ref, x_ref, wg0_ref, wg1_ref, wg2_ref, bg0_ref, bg1_ref, bg2_ref,
                       wa_ref, wb_ref, wc_ref, wo_ref, gmlp_ref, o_ref, h2_ref):
    j = pl.program_id(1)
    h = h_ref[...]

    def gate(wg_ref, bg_ref):
        return jax.nn.sigmoid(jnp.dot(h, wg_ref[...], preferred_element_type=F32) + bg_ref[...])

    ya = jnp.dot(acts_ref[:, 0:D_A], wa_ref[...], preferred_element_type=F32)
    yb = jnp.dot(acts_ref[:, ACT_B:ACT_B + D_B], wb_ref[...], preferred_element_type=F32)
    yc = jnp.dot(acts_ref[:, ACT_C:ACT_C + D_C], wc_ref[...], preferred_element_type=F32)
    merged = gate(wg0_ref, bg0_ref) * ya + gate(wg1_ref, bg1_ref) * yb + gate(wg2_ref, bg2_ref) * yc

    @pl.when(j == 0)
    def _():
        o_ref[...] = x_ref[...]

    o_ref[...] += jnp.dot(merged.astype(BF16), wo_ref[...], preferred_element_type=F32)

    @pl.when(j == pl.num_programs(1) - 1)
    def _():
        h2_ref[...] = _rmsnorm(o_ref[...], gmlp_ref[...]).astype(BF16)


def _gate_merge(h, acts, x, wg, bg, wa, wb, wc, wo, gmlp, *, tm, cb):
    m, d = x.shape
    n_cb = d // cb
    tile = lambda i, j: (i, 0)
    col = lambda i, j: (0, j)
    gate_col = lambda k: (lambda i, j: (0, k * n_cb + j))
    return pl.pallas_call(
        _gate_merge_kernel,
        grid=(m // tm, n_cb),
        in_specs=[
            pl.BlockSpec((tm, d), tile),
            pl.BlockSpec((tm, D_ACTS), tile),
            pl.BlockSpec((tm, d), tile),
            pl.BlockSpec((d, cb), gate_col(0)),
            pl.BlockSpec((d, cb), gate_col(1)),
            pl.BlockSpec((d, cb), gate_col(2)),
            pl.BlockSpec((1, cb), gate_col(0)),
            pl.BlockSpec((1, cb), gate_col(1)),
            pl.BlockSpec((1, cb), gate_col(2)),
            pl.BlockSpec((D_A, cb), col),
            pl.BlockSpec((D_B, cb), col),
            pl.BlockSpec((D_C, cb), col),
            pl.BlockSpec((cb, d), lambda i, j: (j, 0)),
            pl.BlockSpec((1, d), lambda i, j: (0, 0)),
        ],
        out_specs=[pl.BlockSpec((tm, d), tile), pl.BlockSpec((tm, d), tile)],
        out_shape=[jax.ShapeDtypeStruct((m, d), F32), jax.ShapeDtypeStruct((m, d), BF16)],
        compiler_params=_params("arbitrary", "arbitrary"),
        name="gate_merge",
    )(h, acts, x, wg, wg, wg, bg, bg, bg, wa, wb, wc, wo, gmlp)


def _mlp_kernel(h2_ref, x_ref, wu_ref, wd_ref, gf_ref, o_ref, *, final):
    f = pl.program_id(1)

    @pl.when(f == 0)
    def _():
        o_ref[...] = x_ref[...]

    a = jnp.dot(h2_ref[...], wu_ref[...], preferred_element_type=F32)
    a = jnp.square(jnp.maximum(a, 0.0)).astype(BF16)
    o_ref[...] += jnp.dot(a, wd_ref[...], preferred_element_type=F32)

    if final:
        @pl.when(f == pl.num_programs(1) - 1)
        def _():
            o_ref[...] = _rmsnorm(o_ref[...], gf_ref[...])


def _mlp(h2, x, wu, wd, gf, *, final, tm, tf):
    m, d = x.shape
    ff = wu.shape[1]
    tile = lambda i, f: (i, 0)
    return pl.pallas_call(
        functools.partial(_mlp_kernel, final=final),
        grid=(m // tm, ff // tf),
        in_specs=[
            pl.BlockSpec((tm, d), tile),
            pl.BlockSpec((tm, d), tile),
            pl.BlockSpec((d, tf), lambda i, f: (0, f)),
            pl.BlockSpec((tf, d), lambda i, f: (f, 0)),
            pl.BlockSpec((1, d), lambda i, f: (0, 0)),
        ],
        out_specs=pl.BlockSpec((tm, d), tile),
        out_shape=jax.ShapeDtypeStruct((m, d), F32),
        compiler_params=_params("arbitrary", "arbitrary"),
        name="mlp_final" if final else "mlp",
    )(h2, x, wu, wd, gf)


def kernel(x, g_mix, w_in, w_gate, b_gate, b_glu, w_dw_a, b_dw_a, ln_g_a, ln_b_a, w_a_out, w_pool_grp, pool_scale,
           w_b_out, w_dw_c, w_c_out, w_o, g_mlp, w_up, w_down, g_final):
    b, s, d = x.shape
    assert (d, w_in.shape[-1], w_up.shape[-1]) == (D_MODEL, D_IN, D_FF)
    xf = x.reshape(b * s, d)
    row = lambda v: v.reshape(1, -1)
    gf = row(g_final)
    for l in range(DEPTH):
        b_in = jnp.concatenate([b_glu[l], jnp.zeros((D_IN - 2 * D_A,), F32)]).reshape(1, D_IN)
        wdwa8 = jnp.broadcast_to(w_dw_a[l][:, None, :], (K_A, SUBLANES, D_A))
        h, acts = _in_mix(xf, row(g_mix[l]), w_in[l].astype(BF16), b_in, wdwa8, row(b_dw_a[l]), row(ln_g_a[l]),
                          row(ln_b_a[l]), w_pool_grp[l].astype(BF16), row(pool_scale[l]), w_dw_c[l], seq=s, tm=256)
        xf, h2 = _gate_merge(h, acts, xf, w_gate[l].astype(BF16), row(b_gate[l]), w_a_out[l].astype(BF16),
                             w_b_out[l].astype(BF16), w_c_out[l].astype(BF16), w_o[l].astype(BF16), row(g_mlp[l]),
                             tm=512, cb=512)
        xf = _mlp(h2, xf, w_up[l].astype(BF16), w_down[l].astype(BF16), gf, final=(l == DEPTH - 1), tm=1024, tf=512)
    return xf.reshape(b, s, d)
```

```python
import functools

import jax
import jax.numpy as jnp
from jax import lax
from jax.experimental import pallas as pl
from jax.experimental.pallas import tpu as pltpu

D_MODEL = 2048
DEPTH = 2
D_A = 768
K_A = 31
POOL_WINDOWS = (2, 4, 8, 16)
POOL_GROUP = 128
D_B = len(POOL_WINDOWS) * POOL_GROUP
D_C = 768
K_C = 3
D_IN = 2 * D_A + D_B + 3 * D_C
N_BRANCH = 3
D_FF = 4 * D_MODEL
EPS = 1e-6

OFF_AG = D_A
OFF_B = 2 * D_A
OFF_GB = OFF_B + D_B
OFF_GC = OFF_GB + D_C
OFF_XV = OFF_GC + D_C
D_ACTS = D_A + D_B + D_C
ACT_B = D_A
ACT_C = D_A + D_B

SUBLANES = 8
LANES = 128
HALO_A = 32
HALO_B = 16
HALO_C = 8
CONV_GROUPS = 8
Z_BLOCK = 256
Z_CHUNK = 2 * Z_BLOCK

V7X_VMEM_LIMIT = 60 * 1024 * 1024

BF16 = jnp.bfloat16
F32 = jnp.float32


def _rmsnorm(x, g):
    return x * lax.rsqrt(jnp.mean(x * x, axis=-1, keepdims=True) + EPS) * g


def _params(*sem, flags=None):
    return pltpu.CompilerParams(dimension_semantics=sem, vmem_limit_bytes=V7X_VMEM_LIMIT, flags=flags)


N_MIX_IN = 11
N_CAST = 5


def _in_mix_kernel(*refs, tm, tiles_per_seq):
    (x_ref, g_ref, win_ref, bin_ref, wdwa_ref, bdwa_ref, lng_ref, lnb_ref, wpool_ref, pscale_ref,
     wdwc_ref) = refs[:N_MIX_IN]
    h_ref, acts_ref = refs[N_MIX_IN + N_CAST:N_MIX_IN + N_CAST + 2]
    cast_refs = refs[N_MIX_IN:N_MIX_IN + N_CAST] + refs[N_MIX_IN + N_CAST + 2:N_MIX_IN + 2 * N_CAST + 2]
    ua_next, zb_next, gb_next, p_next, gb_cur, ua_ext, cv_buf, zb_ext, p_ext = refs[N_MIX_IN + 2 * N_CAST + 2:]
    t = pl.program_id(0)
    s_prev = lax.rem(t + tiles_per_seq - 1, tiles_per_seq)

    @pl.when(t == 0)
    def _():
        ua_next[...] = jnp.zeros_like(ua_next)
        zb_next[...] = jnp.zeros_like(zb_next)
        gb_next[...] = jnp.zeros_like(gb_next)
        p_next[...] = jnp.zeros_like(p_next)

    @pl.when(jnp.logical_or(t == 0, s_prev == 0))
    def _():
        ua_ext[0:HALO_A, :] = jnp.zeros((HALO_A, D_A), F32)
        zb_ext[0:HALO_B, :] = jnp.zeros((HALO_B, D_B), F32)
        p_ext[0:HALO_C, :] = jnp.zeros((HALO_C, D_C), F32)

    ua_ext[HALO_A:HALO_A + tm, :] = ua_next[...]
    zb_ext[HALO_B:HALO_B + tm, :] = zb_next[...]
    p_ext[HALO_C:HALO_C + tm, :] = p_next[...]
    gb_cur[...] = gb_next[...]

    h_ref[...] = _rmsnorm(x_ref[...], g_ref[...]).astype(BF16)

    def put_block(lo, zc):
        if lo < OFF_AG:
            ua_next[:, lo:lo + Z_BLOCK] = zc
        elif lo < OFF_B:
            c = lo - OFF_AG
            ua_next[:, c:c + Z_BLOCK] = ua_next[:, c:c + Z_BLOCK] * jax.nn.sigmoid(zc)
        elif lo < OFF_GB:
            zb_next[:, lo - OFF_B:lo - OFF_B + Z_BLOCK] = zc
        elif lo < OFF_GC:
            gb_next[:, lo - OFF_GB:lo - OFF_GB + Z_BLOCK] = zc
        elif lo < OFF_XV:
            p_next[:, lo - OFF_GC:lo - OFF_GC + Z_BLOCK] = zc
        else:
            c = lo - OFF_XV
            p_next[:, c:c + Z_BLOCK] = p_next[:, c:c + Z_BLOCK] * zc

    def project(lo):
        hi = min(lo + Z_CHUNK, D_IN)
        zc = jnp.dot(h_ref[...], win_ref[:, lo:hi], preferred_element_type=F32) + bin_ref[:, lo:hi]
        for b in range(0, hi - lo, Z_BLOCK):
            put_block(lo + b, zc[:, b:b + Z_BLOCK])

    rows = CONV_GROUPS * SUBLANES
    row_in_vreg = lax.broadcasted_iota(jnp.int32, (SUBLANES, LANES), 0)

    def conv_block(base, c):
        lanes = slice(c * LANES, (c + 1) * LANES)
        lo = base
        n_win = HALO_A // SUBLANES + CONV_GROUPS
        win = [ua_ext[lo + SUBLANES * j:lo + SUBLANES * (j + 1), lanes] for j in range(n_win)]
        acc = [jnp.zeros((SUBLANES, LANES), F32) for _ in range(CONV_GROUPS)]
        for r in range(SUBLANES):
            if r == 0:
                moved = win
            else:
                rolled = [pltpu.roll(v, r, 0) for v in win]
                moved = [None] + [jnp.where(row_in_vreg >= r, rolled[j], rolled[j - 1]) for j in range(1, n_win)]
            for q in range(HALO_A // SUBLANES):
                lag = SUBLANES * q + r
                if lag >= K_A:
                    continue
                wt = wdwa_ref[K_A - 1 - lag, :, lanes]
                for gi in range(CONV_GROUPS):
                    acc[gi] = acc[gi] + moved[HALO_A // SUBLANES + gi - q] * wt
        cv_buf[base:base + rows, lanes] = jnp.concatenate(acc, axis=0)

    def norm_rows(base):
        va = cv_buf[base:base + rows, :] + bdwa_ref[...]
        mu = jnp.mean(va, axis=-1, keepdims=True)
        xc = va - mu
        var = jnp.mean(xc * xc, axis=-1, keepdims=True)
        y = xc * lax.rsqrt(var + EPS) * lng_ref[...] + lnb_ref[...]
        y = y * jax.nn.sigmoid(y)
        acts_ref[base:base + rows, 0:D_A] = y.astype(BF16)

    def pool_group(gi):
        w = POOL_WINDOWS[gi]
        lo = gi * POOL_GROUP
        seen = (s_prev * tm + 1 + lax.broadcasted_iota(jnp.int32, (tm, 1), 0)).astype(F32)
        tok = zb_ext[HALO_B:HALO_B + tm, lo:lo + POOL_GROUP]
        win = tok
        for lag in range(1, w):
            win = win + zb_ext[HALO_B - lag:HALO_B - lag + tm, lo:lo + POOL_GROUP]
        pooled = win / jnp.minimum(seen, float(w)) - tok
        mixed = jnp.dot(pooled.astype(BF16), wpool_ref[gi], preferred_element_type=F32)
        acts_ref[:, ACT_B + lo:ACT_B + lo + POOL_GROUP] = (mixed * pscale_ref[:, lo:lo + POOL_GROUP]).astype(BF16)

    def short_conv():
        cv = p_ext[HALO_C:HALO_C + tm, :] * wdwc_ref[K_C - 1:K_C, :]
        for lag in range(1, K_C):
            tap = K_C - 1 - lag
            cv = cv + p_ext[HALO_C - lag:HALO_C - lag + tm, :] * wdwc_ref[tap:tap + 1, :]
        acts_ref[:, ACT_C:ACT_C + D_C] = (gb_cur[...] * cv).astype(BF16)

    def keep_tails():
        ua_ext[0:HALO_A, :] = ua_ext[tm:tm + HALO_A, :]
        zb_ext[0:HALO_B, :] = zb_ext[tm:tm + HALO_B, :]
        p_ext[0:HALO_C, :] = p_ext[tm:tm + HALO_C, :]

    mixer_work = []
    for base in range(0, tm, rows):
        mixer_work += [functools.partial(conv_block, base, c) for c in range(D_A // LANES)]
        mixer_work += [functools.partial(norm_rows, base)]
    mixer_work += ([functools.partial(pool_group, gi) for gi in range(len(POOL_WINDOWS))]
                  + [short_conv, keep_tails])
    chunks = list(range(0, D_IN, Z_CHUNK))
    per_chunk = -(-len(mixer_work) // (len(chunks) - 1))
    for ci, lo in enumerate(chunks):
        project(lo)
        for work in mixer_work[ci * per_chunk:(ci + 1) * per_chunk]:
            work()

    for src_ref, dst_ref in zip(cast_refs[:N_CAST], cast_refs[N_CAST:]):
        dst_ref[...] = src_ref[...].astype(BF16)


def _in_mix(x, g, win, b_in, wdwa8, bdwa, lng, lnb, wpool, pscale, wdwc, cast_stacks, cast_layers, *, seq, tm):
    m, d = x.shape
    n_t = m // tm
    cur = lambda t: (jnp.minimum(t, n_t - 1), 0)
    cast_in, cast_out, cast_shape = [], [], []
    for w, layer in zip(cast_stacks, cast_layers):
        _, r, c = w.shape
        assert r % n_t == 0
        cast_in.append(pl.BlockSpec((None, r // n_t, c), lambda t, layer=layer: (layer, jnp.minimum(t, n_t - 1), 0)))
        cast_out.append(pl.BlockSpec((r // n_t, c), cur))
        cast_shape.append(jax.ShapeDtypeStruct((r, c), BF16))
    prev = lambda t: (jnp.maximum(t - 1, 0), 0)
    const2 = lambda t: (0, 0)
    const3 = lambda t: (0, 0, 0)
    resident = functools.partial(pl.BlockSpec, pipeline_mode=pl.Buffered(1))
    return pl.pallas_call(
        functools.partial(_in_mix_kernel, tm=tm, tiles_per_seq=seq // tm),
        grid=(n_t + 1,),
        in_specs=[
            pl.BlockSpec((tm, d), cur),
            resident((1, d), const2),
            resident((d, D_IN), const2),
            resident((1, D_IN), const2),
            resident((K_A, SUBLANES, D_A), const3),
            resident((1, D_A), const2),
            resident((1, D_A), const2),
            resident((1, D_A), const2),
            resident((len(POOL_WINDOWS), POOL_GROUP, POOL_GROUP), const3),
            resident((1, D_B), const2),
            resident((K_C, D_C), const2),
        ] + cast_in,
        out_specs=[pl.BlockSpec((tm, d), cur), pl.BlockSpec((tm, D_ACTS), prev)] + cast_out,
        out_shape=[jax.ShapeDtypeStruct((m, d), BF16), jax.ShapeDtypeStruct((m, D_ACTS), BF16)] + cast_shape,
        scratch_shapes=[
            pltpu.VMEM((tm, D_A), F32),
            pltpu.VMEM((tm, D_B), F32),
            pltpu.VMEM((tm, D_C), F32),
            pltpu.VMEM((tm, D_C), F32),
            pltpu.VMEM((tm, D_C), F32),
            pltpu.VMEM((HALO_A + tm, D_A), F32),
            pltpu.VMEM((tm, D_A), F32),
            pltpu.VMEM((HALO_B + tm, D_B), F32),
            pltpu.VMEM((HALO_C + tm, D_C), F32),
        ],
        compiler_params=_params("arbitrary"),
        name="in_mix",
    )(x, g, win, b_in, wdwa8, bdwa, lng, lnb, wpool, pscale, wdwc, *cast_stacks)


def _gate_merge_kernel(h_ref, acts_ref, x_ref, wg0_ref, wg1_ref, wg2_ref, bg0_ref, bg1_ref, bg2_ref,
                       wa_ref, wb_ref, wc_ref, wo_ref, gmlp_ref, o_ref, h2_ref):
    j = pl.program_id(1)
    h = h_ref[...]

    def gate(wg_ref, bg_ref):
        return jax.nn.sigmoid(jnp.dot(h, wg_ref[...], preferred_element_type=F32) + bg_ref[...])

    ya = jnp.dot(acts_ref[:, 0:D_A], wa_ref[...], preferred_element_type=F32)
    yb = jnp.dot(acts_ref[:, ACT_B:ACT_B + D_B], wb_ref[...], preferred_element_type=F32)
    yc = jnp.dot(acts_ref[:, ACT_C:ACT_C + D_C], wc_ref[...], preferred_element_type=F32)
    merged = gate(wg0_ref, bg0_ref) * ya + gate(wg1_ref, bg1_ref) * yb + gate(wg2_ref, bg2_ref) * yc

    @pl.when(j == 0)
    def _():
        o_ref[...] = x_ref[...]

    o_ref[...] += jnp.dot(merged.astype(BF16), wo_ref[...], preferred_element_type=F32)

    @pl.when(j == pl.num_programs(1) - 1)
    def _():
        h2_ref[...] = _rmsnorm(o_ref[...], gmlp_ref[...]).astype(BF16)


def _gate_merge(h, acts, x, wg, bg, wa, wb, wc, wo, gmlp, *, tm, cb):
    m, d = x.shape
    n_cb = d // cb
    tile = lambda i, j: (i, 0)
    col = lambda i, j: (0, j)
    gate_col = lambda k: (lambda i, j: (0, k * n_cb + j))
    return pl.pallas_call(
        _gate_merge_kernel,
        grid=(m // tm, n_cb),
        in_specs=[
            pl.BlockSpec((tm, d), tile),
            pl.BlockSpec((tm, D_ACTS), tile),
            pl.BlockSpec((tm, d), tile),
            pl.BlockSpec((d, cb), gate_col(0)),
            pl.BlockSpec((d, cb), gate_col(1)),
            pl.BlockSpec((d, cb), gate_col(2)),
            pl.BlockSpec((1, cb), gate_col(0)),
            pl.BlockSpec((1, cb), gate_col(1)),
            pl.BlockSpec((1, cb), gate_col(2)),
            pl.BlockSpec((D_A, cb), col),
            pl.BlockSpec((D_B, cb), col),
            pl.BlockSpec((D_C, cb), col),
            pl.BlockSpec((cb, d), lambda i, j: (j, 0)),
            pl.BlockSpec((1, d), lambda i, j: (0, 0)),
        ],
        out_specs=[pl.BlockSpec((tm, d), tile), pl.BlockSpec((tm, d), tile)],
        out_shape=[jax.ShapeDtypeStruct((m, d), F32), jax.ShapeDtypeStruct((m, d), BF16)],
        compiler_params=_params("arbitrary", "arbitrary"),
        name="gate_merge",
    )(h, acts, x, wg, wg, wg, bg, bg, bg, wa, wb, wc, wo, gmlp)


def _mlp_kernel(h2_ref, x_ref, wu_ref, wd_ref, gf_ref, o_ref, *, final):
    f = pl.program_id(1)

    @pl.when(f == 0)
    def _():
        o_ref[...] = x_ref[...]

    a = jnp.dot(h2_ref[...], wu_ref[...], preferred_element_type=F32)
    a = jnp.square(jnp.maximum(a, 0.0)).astype(BF16)
    o_ref[...] += jnp.dot(a, wd_ref[...], preferred_element_type=F32)

    if final:
        @pl.when(f == pl.num_programs(1) - 1)
        def _():
            o_ref[...] = _rmsnorm(o_ref[...], gf_ref[...])


def _mlp(h2, x, wu, wd, gf, *, final, tm, tf):
    m, d = x.shape
    ff = wu.shape[1]
    tile = lambda i, f: (i, 0)
    return pl.pallas_call(
        functools.partial(_mlp_kernel, final=final),
        grid=(m // tm, ff // tf),
        in_specs=[
            pl.BlockSpec((tm, d), tile),
            pl.BlockSpec((tm, d), tile),
            pl.BlockSpec((d, tf), lambda i, f: (0, f)),
            pl.BlockSpec((tf, d), lambda i, f: (f, 0)),
            pl.BlockSpec((1, d), lambda i, f: (0, 0)),
        ],
        out_specs=pl.BlockSpec((tm, d), tile),
        out_shape=jax.ShapeDtypeStruct((m, d), F32),
        compiler_params=_params("arbitrary", "arbitrary"),
        name="mlp_final" if final else "mlp",
    )(h2, x, wu, wd, gf)


def kernel(x, g_mix, w_in, w_gate, b_gate, b_glu, w_dw_a, b_dw_a, ln_g_a, ln_b_a, w_a_out, w_pool_grp, pool_scale,
           w_b_out, w_dw_c, w_c_out, w_o, g_mlp, w_up, w_down, g_final):
    b, s, d = x.shape
    assert (d, w_in.shape[-1], w_up.shape[-1]) == (D_MODEL, D_IN, D_FF)
    xf = x.reshape(b * s, d)
    row = lambda v: v.reshape(1, -1)
    gf = row(g_final)
    win16 = w_in[0].astype(BF16)
    for l in range(DEPTH):
        b_in = jnp.concatenate([b_glu[l], jnp.zeros((D_IN - 2 * D_A,), F32)]).reshape(1, D_IN)
        wdwa8 = jnp.broadcast_to(w_dw_a[l][:, None, :], (K_A, SUBLANES, D_A))
        nxt = (l + 1) % DEPTH
        h, acts, wg16, wu16, wd16, wo16, win16_next = _in_mix(
            xf, row(g_mix[l]), win16, b_in, wdwa8, row(b_dw_a[l]), row(ln_g_a[l]), row(ln_b_a[l]),
            w_pool_grp[l].astype(BF16), row(pool_scale[l]), w_dw_c[l],
            (w_gate, w_up, w_down, w_o, w_in), (l, l, l, l, nxt), seq=s, tm=256)
        xf, h2 = _gate_merge(h, acts, xf, wg16, row(b_gate[l]), w_a_out[l].astype(BF16), w_b_out[l].astype(BF16),
                             w_c_out[l].astype(BF16), wo16, row(g_mlp[l]), tm=512, cb=512)
        xf = _mlp(h2, xf, wu16, wd16, gf, final=(l == DEPTH - 1), tm=1024, tf=512)
        win16 = win16_next
    return xf.reshape(b, s, d)
```

```python
import functools

import jax
import jax.numpy as jnp
from jax import lax
from jax.experimental import pallas as pl
from jax.experimental.pallas import tpu as pltpu

D_MODEL = 2048
DEPTH = 2
D_A = 768
K_A = 31
POOL_WINDOWS = (2, 4, 8, 16)
POOL_GROUP = 128
D_B = len(POOL_WINDOWS) * POOL_GROUP
D_C = 768
K_C = 3
D_IN = 2 * D_A + D_B + 3 * D_C
N_BRANCH = 3
D_FF = 4 * D_MODEL
EPS = 1e-6

OFF_AG = D_A
OFF_B = 2 * D_A
OFF_GB = OFF_B + D_B
OFF_GC = OFF_GB + D_C
OFF_XV = OFF_GC + D_C
D_ACTS = D_A + D_B + D_C
ACT_B = D_A
ACT_C = D_A + D_B

SUBLANES = 8
LANES = 128
HALO_A = 32
HALO_B = 16
HALO_C = 8
CONV_GROUPS = 4
Z_BLOCK = 256
Z_CHUNK = 2 * Z_BLOCK

V7X_VMEM_LIMIT = 60 * 1024 * 1024

BF16 = jnp.bfloat16
F32 = jnp.float32


def _rmsnorm(x, g):
    return x * lax.rsqrt(jnp.mean(x * x, axis=-1, keepdims=True) + EPS) * g


def _params(*sem, flags=None):
    return pltpu.CompilerParams(dimension_semantics=sem, vmem_limit_bytes=V7X_VMEM_LIMIT, flags=flags)


N_MIX_IN = 11
N_CAST = 5


def _in_mix_kernel(*refs, tm, tiles_per_seq):
    (x_ref, g_ref, win_ref, bin_ref, wdwa_ref, bdwa_ref, lng_ref, lnb_ref, wpool_ref, pscale_ref,
     wdwc_ref) = refs[:N_MIX_IN]
    h_ref, acts_ref = refs[N_MIX_IN + N_CAST:N_MIX_IN + N_CAST + 2]
    cast_refs = refs[N_MIX_IN:N_MIX_IN + N_CAST] + refs[N_MIX_IN + N_CAST + 2:N_MIX_IN + 2 * N_CAST + 2]
    ua_next, zb_next, gb_next, p_next, gb_cur, ua_ext, cv_buf, zb_ext, p_ext = refs[N_MIX_IN + 2 * N_CAST + 2:]
    t = pl.program_id(0)
    s_prev = lax.rem(t + tiles_per_seq - 1, tiles_per_seq)

    @pl.when(t == 0)
    def _():
        ua_next[...] = jnp.zeros_like(ua_next)
        zb_next[...] = jnp.zeros_like(zb_next)
        gb_next[...] = jnp.zeros_like(gb_next)
        p_next[...] = jnp.zeros_like(p_next)

    @pl.when(jnp.logical_or(t == 0, s_prev == 0))
    def _():
        ua_ext[0:HALO_A, :] = jnp.zeros((HALO_A, D_A), F32)
        zb_ext[0:HALO_B, :] = jnp.zeros((HALO_B, D_B), F32)
        p_ext[0:HALO_C, :] = jnp.zeros((HALO_C, D_C), F32)

    ua_ext[HALO_A:HALO_A + tm, :] = ua_next[...]
    zb_ext[HALO_B:HALO_B + tm, :] = zb_next[...]
    p_ext[HALO_C:HALO_C + tm, :] = p_next[...]
    gb_cur[...] = gb_next[...]

    h_ref[...] = _rmsnorm(x_ref[...], g_ref[...]).astype(BF16)

    def put_block(lo, zc):
        if lo < OFF_AG:
            ua_next[:, lo:lo + Z_BLOCK] = zc
        elif lo < OFF_B:
            c = lo - OFF_AG
            ua_next[:, c:c + Z_BLOCK] = ua_next[:, c:c + Z_BLOCK] * jax.nn.sigmoid(zc)
        elif lo < OFF_GB:
            zb_next[:, lo - OFF_B:lo - OFF_B + Z_BLOCK] = zc
        elif lo < OFF_GC:
            gb_next[:, lo - OFF_GB:lo - OFF_GB + Z_BLOCK] = zc
        elif lo < OFF_XV:
            p_next[:, lo - OFF_GC:lo - OFF_GC + Z_BLOCK] = zc
        else:
            c = lo - OFF_XV
            p_next[:, c:c + Z_BLOCK] = p_next[:, c:c + Z_BLOCK] * zc

    def project(lo):
        hi = min(lo + Z_CHUNK, D_IN)
        zc = jnp.dot(h_ref[...], win_ref[:, lo:hi], preferred_element_type=F32) + bin_ref[:, lo:hi]
        for b in range(0, hi - lo, Z_BLOCK):
            put_block(lo + b, zc[:, b:b + Z_BLOCK])
        return zc[0:SUBLANES, 0:LANES]

    rows = CONV_GROUPS * SUBLANES
    row_in_vreg = lax.broadcasted_iota(jnp.int32, (SUBLANES, LANES), 0)
    never = (row_in_vreg + t) < 0

    def conv_block(base, c, tok):
        lanes = slice(c * LANES, (c + 1) * LANES)
        lo = base
        n_win = HALO_A // SUBLANES + CONV_GROUPS
        win = [jnp.where(never, tok, ua_ext[lo + SUBLANES * j:lo + SUBLANES * (j + 1), lanes]) for j in range(n_win)]
        acc = [jnp.zeros((SUBLANES, LANES), F32) for _ in range(CONV_GROUPS)]
        for r in range(SUBLANES):
            if r == 0:
                moved = win
            else:
                rolled = [pltpu.roll(v, r, 0) for v in win]
                moved = [None] + [jnp.where(row_in_vreg >= r, rolled[j], rolled[j - 1]) for j in range(1, n_win)]
            for q in range(HALO_A // SUBLANES):
                lag = SUBLANES * q + r
                if lag >= K_A:
                    continue
                wt = wdwa_ref[K_A - 1 - lag, :, lanes]
                for gi in range(CONV_GROUPS):
                    acc[gi] = acc[gi] + moved[HALO_A // SUBLANES + gi - q] * wt
        cv_buf[base:base + rows, lanes] = jnp.concatenate(acc, axis=0)

    def norm_rows(base, tok):
        va = cv_buf[base:base + rows, :] + bdwa_ref[...]
        mu = jnp.mean(va, axis=-1, keepdims=True)
        xc = va - mu
        var = jnp.mean(xc * xc, axis=-1, keepdims=True)
        y = xc * lax.rsqrt(var + EPS) * lng_ref[...] + lnb_ref[...]
        y = y * jax.nn.sigmoid(y)
        acts_ref[base:base + rows, 0:D_A] = y.astype(BF16)

    def pool_group(gi, tok):
        w = POOL_WINDOWS[gi]
        lo = gi * POOL_GROUP
        seen = (s_prev * tm + 1 + lax.broadcasted_iota(jnp.int32, (tm, 1), 0)).astype(F32)
        tok = zb_ext[HALO_B:HALO_B + tm, lo:lo + POOL_GROUP]
        win = tok
        for lag in range(1, w):
            win = win + zb_ext[HALO_B - lag:HALO_B - lag + tm, lo:lo + POOL_GROUP]
        pooled = win / jnp.minimum(seen, float(w)) - tok
        mixed = jnp.dot(pooled.astype(BF16), wpool_ref[gi], preferred_element_type=F32)
        acts_ref[:, ACT_B + lo:ACT_B + lo + POOL_GROUP] = (mixed * pscale_ref[:, lo:lo + POOL_GROUP]).astype(BF16)

    def short_conv(tok):
        cv = p_ext[HALO_C:HALO_C + tm, :] * wdwc_ref[K_C - 1:K_C, :]
        for lag in range(1, K_C):
            tap = K_C - 1 - lag
            cv = cv + p_ext[HALO_C - lag:HALO_C - lag + tm, :] * wdwc_ref[tap:tap + 1, :]
        acts_ref[:, ACT_C:ACT_C + D_C] = (gb_cur[...] * cv).astype(BF16)

    def keep_tails(tok):
        ua_ext[0:HALO_A, :] = ua_ext[tm:tm + HALO_A, :]
        zb_ext[0:HALO_B, :] = zb_ext[tm:tm + HALO_B, :]
        p_ext[0:HALO_C, :] = p_ext[tm:tm + HALO_C, :]

    mixer_work = []
    for base in range(0, tm, rows):
        mixer_work += [functools.partial(conv_block, base, c) for c in range(D_A // LANES)]
        mixer_work += [functools.partial(norm_rows, base)]
    mixer_work += ([functools.partial(pool_group, gi) for gi in range(len(POOL_WINDOWS))]
                  + [short_conv, keep_tails])
    chunks = list(range(0, D_IN, Z_CHUNK))
    per_chunk = -(-len(mixer_work) // (len(chunks) - 1))
    for ci, lo in enumerate(chunks):
        tok = project(lo)
        for work in mixer_work[ci * per_chunk:(ci + 1) * per_chunk]:
            work(tok)

    for src_ref, dst_ref in zip(cast_refs[:N_CAST], cast_refs[N_CAST:]):
        dst_ref[...] = src_ref[...].astype(BF16)


def _in_mix(x, g, win, b_in, wdwa8, bdwa, lng, lnb, wpool, pscale, wdwc, cast_stacks, cast_layers, *, seq, tm):
    m, d = x.shape
    n_t = m // tm
    cur = lambda t: (jnp.minimum(t, n_t - 1), 0)
    cast_in, cast_out, cast_shape = [], [], []
    for w, layer in zip(cast_stacks, cast_layers):
        _, r, c = w.shape
        assert r % n_t == 0
        cast_in.append(pl.BlockSpec((None, r // n_t, c), lambda t, layer=layer: (layer, jnp.minimum(t, n_t - 1), 0)))
        cast_out.append(pl.BlockSpec((r // n_t, c), cur))
        cast_shape.append(jax.ShapeDtypeStruct((r, c), BF16))
    prev = lambda t: (jnp.maximum(t - 1, 0), 0)
    const2 = lambda t: (0, 0)
    const3 = lambda t: (0, 0, 0)
    resident = functools.partial(pl.BlockSpec, pipeline_mode=pl.Buffered(1))
    return pl.pallas_call(
        functools.partial(_in_mix_kernel, tm=tm, tiles_per_seq=seq // tm),
        grid=(n_t + 1,),
        in_specs=[
            pl.BlockSpec((tm, d), cur),
            resident((1, d), const2),
            resident((d, D_IN), const2),
            resident((1, D_IN), const2),
            resident((K_A, SUBLANES, D_A), const3),
            resident((1, D_A), const2),
            resident((1, D_A), const2),
            resident((1, D_A), const2),
            resident((len(POOL_WINDOWS), POOL_GROUP, POOL_GROUP), const3),
            resident((1, D_B), const2),
            resident((K_C, D_C), const2),
        ] + cast_in,
        out_specs=[pl.BlockSpec((tm, d), cur), pl.BlockSpec((tm, D_ACTS), prev)] + cast_out,
        out_shape=[jax.ShapeDtypeStruct((m, d), BF16), jax.ShapeDtypeStruct((m, D_ACTS), BF16)] + cast_shape,
        scratch_shapes=[
            pltpu.VMEM((tm, D_A), F32),
            pltpu.VMEM((tm, D_B), F32),
            pltpu.VMEM((tm, D_C), F32),
            pltpu.VMEM((tm, D_C), F32),
            pltpu.VMEM((tm, D_C), F32),
            pltpu.VMEM((HALO_A + tm, D_A), F32),
            pltpu.VMEM((tm, D_A), F32),
            pltpu.VMEM((HALO_B + tm, D_B), F32),
            pltpu.VMEM((HALO_C + tm, D_C), F32),
        ],
        compiler_params=_params("arbitrary"),
        name="in_mix",
    )(x, g, win, b_in, wdwa8, bdwa, lng, lnb, wpool, pscale, wdwc, *cast_stacks)


def _gate_merge_kernel(h_ref, acts_ref, x_ref, wg0_ref, wg1_ref, wg2_ref, bg0_ref, bg1_ref, bg2_ref,
                       wa_ref, wb_ref, wc_ref, wo_ref, gmlp_ref, o_ref, h2_ref):
    j = pl.program_id(1)
    h = h_ref[...]

    def gate(wg_ref, bg_ref):
        return jax.nn.sigmoid(jnp.dot(h, wg_ref[...], preferred_element_type=F32) + bg_ref[...])

    ya = jnp.dot(acts_ref[:, 0:D_A], wa_ref[...], preferred_element_type=F32)
    yb = jnp.dot(acts_ref[:, ACT_B:ACT_B + D_B], wb_ref[...], preferred_element_type=F32)
    yc = jnp.dot(acts_ref[:, ACT_C:ACT_C + D_C], wc_ref[...], preferred_element_type=F32)
    merged = gate(wg0_ref, bg0_ref) * ya + gate(wg1_ref, bg1_ref) * yb + gate(wg2_ref, bg2_ref) * yc

    @pl.when(j == 0)
    def _():
        o_ref[...] = x_ref[...]

    o_ref[...] += jnp.dot(merged.astype(BF16), wo_ref[...], preferred_element_type=F32)

    @pl.when(j == pl.num_programs(1) - 1)
    def _():
        h2_ref[...] = _rmsnorm(o_ref[...], gmlp_ref[...]).astype(BF16)


def _gate_merge(h, acts, x, wg, bg, wa, wb, wc, wo, gmlp, *, tm, cb):
    m, d = x.shape
    n_cb = d // cb
    tile = lambda i, j: (i, 0)
    col = lambda i, j: (0, j)
    gate_col = lambda k: (lambda i, j: (0, k * n_cb + j))
    return pl.pallas_call(
        _gate_merge_kernel,
        grid=(m // tm, n_cb),
        in_specs=[
            pl.BlockSpec((tm, d), tile),
            pl.BlockSpec((tm, D_ACTS), tile),
            pl.BlockSpec((tm, d), tile),
            pl.BlockSpec((d, cb), gate_col(0)),
            pl.BlockSpec((d, cb), gate_col(1)),
            pl.BlockSpec((d, cb), gate_col(2)),
            pl.BlockSpec((1, cb), gate_col(0)),
            pl.BlockSpec((1, cb), gate_col(1)),
            pl.BlockSpec((1, cb), gate_col(2)),
            pl.BlockSpec((D_A, cb), col),
            pl.BlockSpec((D_B, cb), col),
            pl.BlockSpec((D_C, cb), col),
            pl.BlockSpec((cb, d), lambda i, j: (j, 0)),
            pl.BlockSpec((1, d), lambda i, j: (0, 0)),
        ],
        out_specs=[pl.BlockSpec((tm, d), tile), pl.BlockSpec((tm, d), tile)],
        out_shape=[jax.ShapeDtypeStruct((m, d), F32), jax.ShapeDtypeStruct((m, d), BF16)],
        compiler_params=_params("arbitrary", "arbitrary"),
        name="gate_merge",
    )(h, acts, x, wg, wg, wg, bg, bg, bg, wa, wb, wc, wo, gmlp)


def _mlp_kernel(h2_ref, x_ref, wu_ref, wd_ref, gf_ref, o_ref, *, final):
    f = pl.program_id(1)

    @pl.when(f == 0)
    def _():
        o_ref[...] = x_ref[...]

    a = jnp.dot(h2_ref[...], wu_ref[...], preferred_element_type=F32)
    a = jnp.square(jnp.maximum(a, 0.0)).astype(BF16)
    o_ref[...] += jnp.dot(a, wd_ref[...], preferred_element_type=F32)

    if final:
        @pl.when(f == pl.num_programs(1) - 1)
        def _():
            o_ref[...] = _rmsnorm(o_ref[...], gf_ref[...])


def _mlp(h2, x, wu, wd, gf, *, final, tm, tf):
    m, d = x.shape
    ff = wu.shape[1]
    tile = lambda i, f: (i, 0)
    return pl.pallas_call(
        functools.partial(_mlp_kernel, final=final),
        grid=(m // tm, ff // tf),
        in_specs=[
            pl.BlockSpec((tm, d), tile),
            pl.BlockSpec((tm, d), tile),
            pl.BlockSpec((d, tf), lambda i, f: (0, f)),
            pl.BlockSpec((tf, d), lambda i, f: (f, 0)),
            pl.BlockSpec((1, d), lambda i, f: (0, 0)),
        ],
        out_specs=pl.BlockSpec((tm, d), tile),
        out_shape=jax.ShapeDtypeStruct((m, d), F32),
        compiler_params=_params("arbitrary", "arbitrary"),
        name="mlp_final" if final else "mlp",
    )(h2, x, wu, wd, gf)


def kernel(x, g_mix, w_in, w_gate, b_gate, b_glu, w_dw_a, b_dw_a, ln_g_a, ln_b_a, w_a_out, w_pool_grp, pool_scale,
           w_b_out, w_dw_c, w_c_out, w_o, g_mlp, w_up, w_down, g_final):
    b, s, d = x.shape
    assert (d, w_in.shape[-1], w_up.shape[-1]) == (D_MODEL, D_IN, D_FF)
    xf = x.reshape(b * s, d)
    row = lambda v: v.reshape(1, -1)
    gf = row(g_final)
    win16 = w_in[0].astype(BF16)
    for l in range(DEPTH):
        b_in = jnp.concatenate([b_glu[l], jnp.zeros((D_IN - 2 * D_A,), F32)]).reshape(1, D_IN)
        wdwa8 = jnp.broadcast_to(w_dw_a[l][:, None, :], (K_A, SUBLANES, D_A))
        nxt = (l + 1) % DEPTH
        h, acts, wg16, wu16, wd16, wo16, win16_next = _in_mix(
            xf, row(g_mix[l]), win16, b_in, wdwa8, row(b_dw_a[l]), row(ln_g_a[l]), row(ln_b_a[l]),
            w_pool_grp[l].astype(BF16), row(pool_scale[l]), w_dw_c[l],
            (w_gate, w_up, w_down, w_o, w_in), (l, l, l, l, nxt), seq=s, tm=256)
        xf, h2 = _gate_merge(h, acts, xf, wg16, row(b_gate[l]), w_a_out[l].astype(BF16), w_b_out[l].astype(BF16),
                             w_c_out[l].astype(BF16), wo16, row(g_mlp[l]), tm=512, cb=512)
        xf = _mlp(h2, xf, wu16, wd16, gf, final=(l == DEPTH - 1), tm=1024, tf=512)
        win16 = win16_next
    return xf.reshape(b, s, d)
```

```python
import functools

import jax
import jax.numpy as jnp
from jax import lax
from jax.experimental import pallas as pl
from jax.experimental.pallas import tpu as pltpu

D_MODEL = 2048
DEPTH = 2
D_A = 768
K_A = 31
POOL_WINDOWS = (2, 4, 8, 16)
POOL_GROUP = 128
D_B = len(POOL_WINDOWS) * POOL_GROUP
D_C = 768
K_C = 3
D_IN = 2 * D_A + D_B + 3 * D_C
N_BRANCH = 3
D_FF = 4 * D_MODEL
EPS = 1e-6

OFF_AG = D_A
OFF_B = 2 * D_A
OFF_GB = OFF_B + D_B
OFF_GC = OFF_GB + D_C
OFF_XV = OFF_GC + D_C
D_ACTS = D_A + D_B + D_C
ACT_B = D_A
ACT_C = D_A + D_B

SUBLANES = 8
LANES = 128
HALO_A = 32
HALO_B = 16
HALO_C = 8
CONV_GROUPS = 4
Z_BLOCK = 256
Z_CHUNK = 2 * Z_BLOCK

V7X_VMEM_LIMIT = 60 * 1024 * 1024

BF16 = jnp.bfloat16
F32 = jnp.float32


def _rmsnorm(x, g):
    return x * lax.rsqrt(jnp.mean(x * x, axis=-1, keepdims=True) + EPS) * g


def _params(*sem, flags=None):
    return pltpu.CompilerParams(dimension_semantics=sem, vmem_limit_bytes=V7X_VMEM_LIMIT, flags=flags)


N_MIX_IN = 11
N_CAST = 5


def _in_mix_kernel(*refs, tm, tiles_per_seq):
    (x_ref, g_ref, win_ref, bin_ref, wdwa_ref, bdwa_ref, lng_ref, lnb_ref, wpool_ref, pscale_ref,
     wdwc_ref) = refs[:N_MIX_IN]
    h_ref, acts_ref = refs[N_MIX_IN + N_CAST:N_MIX_IN + N_CAST + 2]
    cast_refs = refs[N_MIX_IN:N_MIX_IN + N_CAST] + refs[N_MIX_IN + N_CAST + 2:N_MIX_IN + 2 * N_CAST + 2]
    ua_next, zb_next, gb_next, p_next, gb_cur, ua_ext, cv_buf, zb_ext, p_ext = refs[N_MIX_IN + 2 * N_CAST + 2:]
    t = pl.program_id(0)
    s_prev = lax.rem(t + tiles_per_seq - 1, tiles_per_seq)

    @pl.when(t == 0)
    def _():
        ua_next[...] = jnp.zeros_like(ua_next)
        zb_next[...] = jnp.zeros_like(zb_next)
        gb_next[...] = jnp.zeros_like(gb_next)
        p_next[...] = jnp.zeros_like(p_next)

    @pl.when(jnp.logical_or(t == 0, s_prev == 0))
    def _():
        ua_ext[0:HALO_A, :] = jnp.zeros((HALO_A, D_A), F32)
        zb_ext[0:HALO_B, :] = jnp.zeros((HALO_B, D_B), F32)
        p_ext[0:HALO_C, :] = jnp.zeros((HALO_C, D_C), F32)

    ua_ext[HALO_A:HALO_A + tm, :] = ua_next[...]
    zb_ext[HALO_B:HALO_B + tm, :] = zb_next[...]
    p_ext[HALO_C:HALO_C + tm, :] = p_next[...]
    gb_cur[...] = gb_next[...]

    h_ref[...] = _rmsnorm(x_ref[...], g_ref[...]).astype(BF16)

    def put_block(lo, zc):
        if lo < OFF_AG:
            ua_next[:, lo:lo + Z_BLOCK] = zc
        elif lo < OFF_B:
            c = lo - OFF_AG
            ua_next[:, c:c + Z_BLOCK] = ua_next[:, c:c + Z_BLOCK] * jax.nn.sigmoid(zc)
        elif lo < OFF_GB:
            zb_next[:, lo - OFF_B:lo - OFF_B + Z_BLOCK] = zc
        elif lo < OFF_GC:
            gb_next[:, lo - OFF_GB:lo - OFF_GB + Z_BLOCK] = zc
        elif lo < OFF_XV:
            p_next[:, lo - OFF_GC:lo - OFF_GC + Z_BLOCK] = zc
        else:
            c = lo - OFF_XV
            p_next[:, c:c + Z_BLOCK] = p_next[:, c:c + Z_BLOCK] * zc

    def project(lo):
        hi = min(lo + Z_CHUNK, D_IN)
        zc = jnp.dot(h_ref[...], win_ref[:, lo:hi], preferred_element_type=F32) + bin_ref[:, lo:hi]
        for b in range(0, hi - lo, Z_BLOCK):
            put_block(lo + b, zc[:, b:b + Z_BLOCK])
        return zc[0:SUBLANES, 0:LANES]

    rows = CONV_GROUPS * SUBLANES
    row_in_vreg = lax.broadcasted_iota(jnp.int32, (SUBLANES, LANES), 0)
    never = (row_in_vreg + t) < 0

    def conv_block(base, c, tok):
        lanes = slice(c * LANES, (c + 1) * LANES)
        lo = base
        n_win = HALO_A // SUBLANES + CONV_GROUPS
        win = [jnp.where(never, tok, ua_ext[lo + SUBLANES * j:lo + SUBLANES * (j + 1), lanes]) for j in range(n_win)]
        acc = [jnp.zeros((SUBLANES, LANES), F32) for _ in range(CONV_GROUPS)]
        for r in range(SUBLANES):
            if r == 0:
                moved = win
            else:
                rolled = [pltpu.roll(v, r, 0) for v in win]
                moved = [None] + [jnp.where(row_in_vreg >= r, rolled[j], rolled[j - 1]) for j in range(1, n_win)]
            for q in range(HALO_A // SUBLANES):
                lag = SUBLANES * q + r
                if lag >= K_A:
                    continue
                wt = wdwa_ref[K_A - 1 - lag, :, lanes]
                for gi in range(CONV_GROUPS):
                    acc[gi] = acc[gi] + moved[HALO_A // SUBLANES + gi - q] * wt
        cv_buf[base:base + rows, lanes] = jnp.concatenate(acc, axis=0)

    def norm_rows(base, tok):
        va = cv_buf[base:base + rows, :] + bdwa_ref[...]
        mu = jnp.mean(va, axis=-1, keepdims=True)
        xc = va - mu
        var = jnp.mean(xc * xc, axis=-1, keepdims=True)
        y = xc * lax.rsqrt(var + EPS) * lng_ref[...] + lnb_ref[...]
        y = y * jax.nn.sigmoid(y)
        acts_ref[base:base + rows, 0:D_A] = y.astype(BF16)

    def pool_group(gi, tok):
        w = POOL_WINDOWS[gi]
        lo = gi * POOL_GROUP
        seen = (s_prev * tm + 1 + lax.broadcasted_iota(jnp.int32, (tm, 1), 0)).astype(F32)
        tok = zb_ext[HALO_B:HALO_B + tm, lo:lo + POOL_GROUP]
        win = tok
        for lag in range(1, w):
            win = win + zb_ext[HALO_B - lag:HALO_B - lag + tm, lo:lo + POOL_GROUP]
        pooled = win / jnp.minimum(seen, float(w)) - tok
        mixed = jnp.dot(pooled.astype(BF16), wpool_ref[gi], preferred_element_type=F32)
        acts_ref[:, ACT_B + lo:ACT_B + lo + POOL_GROUP] = (mixed * pscale_ref[:, lo:lo + POOL_GROUP]).astype(BF16)

    def short_conv(tok):
        cv = p_ext[HALO_C:HALO_C + tm, :] * wdwc_ref[K_C - 1:K_C, :]
        for lag in range(1, K_C):
            tap = K_C - 1 - lag
            cv = cv + p_ext[HALO_C - lag:HALO_C - lag + tm, :] * wdwc_ref[tap:tap + 1, :]
        acts_ref[:, ACT_C:ACT_C + D_C] = (gb_cur[...] * cv).astype(BF16)

    def keep_tails(tok):
        ua_ext[0:HALO_A, :] = ua_ext[tm:tm + HALO_A, :]
        zb_ext[0:HALO_B, :] = zb_ext[tm:tm + HALO_B, :]
        p_ext[0:HALO_C, :] = p_ext[tm:tm + HALO_C, :]

    mixer_work = []
    for base in range(0, tm, rows):
        mixer_work += [functools.partial(conv_block, base, c) for c in range(D_A // LANES)]
        mixer_work += [functools.partial(norm_rows, base)]
    mixer_work += ([functools.partial(pool_group, gi) for gi in range(len(POOL_WINDOWS))]
                  + [short_conv, keep_tails])
    chunks = list(range(0, D_IN, Z_CHUNK))
    per_chunk = -(-len(mixer_work) // (len(chunks) - 1))
    for ci, lo in enumerate(chunks):
        tok = project(lo)
        for work in mixer_work[ci * per_chunk:(ci + 1) * per_chunk]:
            work(tok)

    for src_ref, dst_ref in zip(cast_refs[:N_CAST], cast_refs[N_CAST:]):
        if len(dst_ref.shape) == 2:
            dst_ref[...] = src_ref[...].astype(BF16)
        else:
            n_blk, _, width = dst_ref.shape
            for k in range(n_blk):
                dst_ref[k] = src_ref[:, k * width:(k + 1) * width].astype(BF16)


def _in_mix(x, g, win, b_in, wdwa8, bdwa, lng, lnb, wpool, pscale, wdwc, cast_stacks, cast_layers, cast_widths, *,
            seq, tm):
    m, d = x.shape
    n_t = m // tm
    cur = lambda t: (jnp.minimum(t, n_t - 1), 0)
    cast_in, cast_out, cast_shape = [], [], []
    for w, layer, width in zip(cast_stacks, cast_layers, cast_widths):
        _, r, c = w.shape
        assert r % n_t == 0
        cast_in.append(pl.BlockSpec((None, r // n_t, c), lambda t, layer=layer: (layer, jnp.minimum(t, n_t - 1), 0)))
        if width is None:
            cast_out.append(pl.BlockSpec((r // n_t, c), cur))
            cast_shape.append(jax.ShapeDtypeStruct((r, c), BF16))
        else:
            cast_out.append(pl.BlockSpec((c // width, r // n_t, width), lambda t: (0, jnp.minimum(t, n_t - 1), 0)))
            cast_shape.append(jax.ShapeDtypeStruct((c // width, r, width), BF16))
    prev = lambda t: (jnp.maximum(t - 1, 0), 0)
    const2 = lambda t: (0, 0)
    const3 = lambda t: (0, 0, 0)
    resident = functools.partial(pl.BlockSpec, pipeline_mode=pl.Buffered(1))
    return pl.pallas_call(
        functools.partial(_in_mix_kernel, tm=tm, tiles_per_seq=seq // tm),
        grid=(n_t + 1,),
        in_specs=[
            pl.BlockSpec((tm, d), cur),
            resident((1, d), const2),
            resident((d, D_IN), const2),
            resident((1, D_IN), const2),
            resident((K_A, SUBLANES, D_A), const3),
            resident((1, D_A), const2),
            resident((1, D_A), const2),
            resident((1, D_A), const2),
            resident((len(POOL_WINDOWS), POOL_GROUP, POOL_GROUP), const3),
            resident((1, D_B), const2),
            resident((K_C, D_C), const2),
        ] + cast_in,
        out_specs=[pl.BlockSpec((tm, d), cur), pl.BlockSpec((tm, D_ACTS), prev)] + cast_out,
        out_shape=[jax.ShapeDtypeStruct((m, d), BF16), jax.ShapeDtypeStruct((m, D_ACTS), BF16)] + cast_shape,
        scratch_shapes=[
            pltpu.VMEM((tm, D_A), F32),
            pltpu.VMEM((tm, D_B), F32),
            pltpu.VMEM((tm, D_C), F32),
            pltpu.VMEM((tm, D_C), F32),
            pltpu.VMEM((tm, D_C), F32),
            pltpu.VMEM((HALO_A + tm, D_A), F32),
            pltpu.VMEM((tm, D_A), F32),
            pltpu.VMEM((HALO_B + tm, D_B), F32),
            pltpu.VMEM((HALO_C + tm, D_C), F32),
        ],
        compiler_params=_params("arbitrary"),
        name="in_mix",
    )(x, g, win, b_in, wdwa8, bdwa, lng, lnb, wpool, pscale, wdwc, *cast_stacks)


def _gate_merge_kernel(h_ref, acts_ref, x_ref, wg0_ref, wg1_ref, wg2_ref, bg0_ref, bg1_ref, bg2_ref,
                       wa_ref, wb_ref, wc_ref, wo_ref, gmlp_ref, o_ref, h2_ref):
    j = pl.program_id(1)
    h = h_ref[...]
    cb = wo_ref.shape[0]

    def merged_cols(cols):
        def gate(wg_ref, bg_ref):
            return jax.nn.sigmoid(jnp.dot(h, wg_ref[:, cols], preferred_element_type=F32) + bg_ref[:, cols])

        g0 = gate(wg0_ref, bg0_ref)
        g1 = gate(wg1_ref, bg1_ref)
        g2 = gate(wg2_ref, bg2_ref)
        mg = g0 * jnp.dot(acts_ref[:, 0:D_A], wa_ref[:, cols], preferred_element_type=F32)
        mg = mg + g1 * jnp.dot(acts_ref[:, ACT_B:ACT_B + D_B], wb_ref[:, cols], preferred_element_type=F32)
        mg = mg + g2 * jnp.dot(acts_ref[:, ACT_C:ACT_C + D_C], wc_ref[:, cols], preferred_element_type=F32)
        return mg.astype(BF16)

    @pl.when(j == 0)
    def _():
        o_ref[...] = x_ref[...]

    halves = [slice(0, cb // 2), slice(cb // 2, cb)]
    merged = [merged_cols(cols) for cols in halves]
    for cols, mg in zip(halves, merged):
        o_ref[...] += jnp.dot(mg, wo_ref[cols, :], preferred_element_type=F32)

    @pl.when(j == pl.num_programs(1) - 1)
    def _():
        h2_ref[...] = _rmsnorm(o_ref[...], gmlp_ref[...]).astype(BF16)


def _gate_merge(h, acts, x, wg, bg, wa, wb, wc, wo, gmlp, *, tm, cb):
    m, d = x.shape
    n_cb = d // cb
    tile = lambda i, j: (i, 0)
    blk = lambda i, j: (j, 0, 0)
    gate_blk = lambda k: (lambda i, j: (k * n_cb + j, 0, 0))
    gate_col = lambda k: (lambda i, j: (0, k * n_cb + j))
    assert wg.shape == (N_BRANCH * n_cb, d, cb) and wa.shape == (n_cb, D_A, cb)
    return pl.pallas_call(
        _gate_merge_kernel,
        grid=(m // tm, n_cb),
        in_specs=[
            pl.BlockSpec((tm, d), tile),
            pl.BlockSpec((tm, D_ACTS), tile),
            pl.BlockSpec((tm, d), tile),
            pl.BlockSpec((None, d, cb), gate_blk(0)),
            pl.BlockSpec((None, d, cb), gate_blk(1)),
            pl.BlockSpec((None, d, cb), gate_blk(2)),
            pl.BlockSpec((1, cb), gate_col(0)),
            pl.BlockSpec((1, cb), gate_col(1)),
            pl.BlockSpec((1, cb), gate_col(2)),
            pl.BlockSpec((None, D_A, cb), blk),
            pl.BlockSpec((None, D_B, cb), blk),
            pl.BlockSpec((None, D_C, cb), blk),
            pl.BlockSpec((cb, d), lambda i, j: (j, 0)),
            pl.BlockSpec((1, d), lambda i, j: (0, 0)),
        ],
        out_specs=[pl.BlockSpec((tm, d), tile), pl.BlockSpec((tm, d), tile)],
        out_shape=[jax.ShapeDtypeStruct((m, d), F32), jax.ShapeDtypeStruct((m, d), BF16)],
        compiler_params=_params("arbitrary", "arbitrary"),
        name="gate_merge",
    )(h, acts, x, wg, wg, wg, bg, bg, bg, wa, wb, wc, wo, gmlp)


def _mlp_kernel(h2_ref, x_ref, wu_ref, wd_ref, gf_ref, o_ref, *, final):
    f = pl.program_id(1)

    @pl.when(f == 0)
    def _():
        o_ref[...] = x_ref[...]

    a = jnp.dot(h2_ref[...], wu_ref[...], preferred_element_type=F32)
    a = jnp.square(jnp.maximum(a, 0.0)).astype(BF16)
    o_ref[...] += jnp.dot(a, wd_ref[...], preferred_element_type=F32)

    if final:
        @pl.when(f == pl.num_programs(1) - 1)
        def _():
            o_ref[...] = _rmsnorm(o_ref[...], gf_ref[...])


def _mlp(h2, x, wu, wd, gf, *, final, tm, tf):
    m, d = x.shape
    ff = wd.shape[0]
    assert wu.shape == (ff // tf, d, tf)
    tile = lambda i, f: (i, 0)
    return pl.pallas_call(
        functools.partial(_mlp_kernel, final=final),
        grid=(m // tm, ff // tf),
        in_specs=[
            pl.BlockSpec((tm, d), tile),
            pl.BlockSpec((tm, d), tile),
            pl.BlockSpec((None, d, tf), lambda i, f: (f, 0, 0)),
            pl.BlockSpec((tf, d), lambda i, f: (f, 0)),
            pl.BlockSpec((1, d), lambda i, f: (0, 0)),
        ],
        out_specs=pl.BlockSpec((tm, d), tile),
        out_shape=jax.ShapeDtypeStruct((m, d), F32),
        compiler_params=_params("arbitrary", "arbitrary"),
        name="mlp_final" if final else "mlp",
    )(h2, x, wu, wd, gf)


def kernel(x, g_mix, w_in, w_gate, b_gate, b_glu, w_dw_a, b_dw_a, ln_g_a, ln_b_a, w_a_out, w_pool_grp, pool_scale,
           w_b_out, w_dw_c, w_c_out, w_o, g_mlp, w_up, w_down, g_final):
    b, s, d = x.shape
    assert (d, w_in.shape[-1], w_up.shape[-1]) == (D_MODEL, D_IN, D_FF)
    xf = x.reshape(b * s, d)
    row = lambda v: v.reshape(1, -1)
    gf = row(g_final)
    win16 = w_in[0].astype(BF16)
    tm_mix, tm_merge, cb, tm_mlp, tf = 256, 512, 512, 1024, 512
    col_blocks = lambda w: w.astype(BF16).reshape(w.shape[0], -1, cb).transpose(1, 0, 2)
    for l in range(DEPTH):
        b_in = jnp.concatenate([b_glu[l], jnp.zeros((D_IN - 2 * D_A,), F32)]).reshape(1, D_IN)
        wdwa8 = jnp.broadcast_to(w_dw_a[l][:, None, :], (K_A, SUBLANES, D_A))
        nxt = (l + 1) % DEPTH
        h, acts, wg16, wu16, wd16, wo16, win16_next = _in_mix(
            xf, row(g_mix[l]), win16, b_in, wdwa8, row(b_dw_a[l]), row(ln_g_a[l]), row(ln_b_a[l]),
            w_pool_grp[l].astype(BF16), row(pool_scale[l]), w_dw_c[l],
            (w_gate, w_up, w_down, w_o, w_in), (l, l, l, l, nxt), (cb, tf, None, None, None), seq=s, tm=tm_mix)
        xf, h2 = _gate_merge(h, acts, xf, wg16, row(b_gate[l]), col_blocks(w_a_out[l]), col_blocks(w_b_out[l]),
                             col_blocks(w_c_out[l]), wo16, row(g_mlp[l]), tm=tm_merge, cb=cb)
        xf = _mlp(h2, xf, wu16, wd16, gf, final=(l == DEPTH - 1), tm=tm_mlp, tf=tf)
        win16 = win16_next
    return xf.reshape(b, s, d)
```

```python
import functools

import jax
import jax.numpy as jnp
from jax import lax
from jax.experimental import pallas as pl
from jax.experimental.pallas import tpu as pltpu

D_MODEL = 2048
DEPTH = 2
D_A = 768
K_A = 31
POOL_WINDOWS = (2, 4, 8, 16)
POOL_GROUP = 128
D_B = len(POOL_WINDOWS) * POOL_GROUP
D_C = 768
K_C = 3
D_IN = 2 * D_A + D_B + 3 * D_C
N_BRANCH = 3
D_FF = 4 * D_MODEL
EPS = 1e-6

OFF_AG = D_A
OFF_B = 2 * D_A
OFF_GB = OFF_B + D_B
OFF_GC = OFF_GB + D_C
OFF_XV = OFF_GC + D_C
D_ACTS = D_A + D_B + D_C
ACT_B = D_A
ACT_C = D_A + D_B

SUBLANES = 8
LANES = 128
HALO_A = 32
HALO_B = 16
HALO_C = 8
CONV_GROUPS = 4
Z_BLOCK = 256
Z_CHUNK = 2 * Z_BLOCK

V7X_VMEM_LIMIT = 60 * 1024 * 1024

BF16 = jnp.bfloat16
F32 = jnp.float32


def _rmsnorm(x, g):
    return x * lax.rsqrt(jnp.mean(x * x, axis=-1, keepdims=True) + EPS) * g


def _params(*sem, flags=None):
    return pltpu.CompilerParams(dimension_semantics=sem, vmem_limit_bytes=V7X_VMEM_LIMIT, flags=flags)


N_MIX_IN = 11
N_CAST = 8
BF16_ROWS = 16


def _in_mix_kernel(*refs, tm, tiles_per_seq):
    (x_ref, g_ref, win_ref, bin_ref, wdwa_ref, bdwa_ref, lng_ref, lnb_ref, wpool_ref, pscale_ref,
     wdwc_ref) = refs[:N_MIX_IN]
    h_ref, acts_ref = refs[N_MIX_IN + N_CAST:N_MIX_IN + N_CAST + 2]
    cast_refs = refs[N_MIX_IN:N_MIX_IN + N_CAST] + refs[N_MIX_IN + N_CAST + 2:N_MIX_IN + 2 * N_CAST + 2]
    ua_next, zb_next, gb_next, p_next, gb_cur, ua_ext, cv_buf, zb_ext, p_ext = refs[N_MIX_IN + 2 * N_CAST + 2:]
    t = pl.program_id(0)
    s_prev = lax.rem(t + tiles_per_seq - 1, tiles_per_seq)

    @pl.when(t == 0)
    def _():
        ua_next[...] = jnp.zeros_like(ua_next)
        zb_next[...] = jnp.zeros_like(zb_next)
        gb_next[...] = jnp.zeros_like(gb_next)
        p_next[...] = jnp.zeros_like(p_next)

    @pl.when(jnp.logical_or(t == 0, s_prev == 0))
    def _():
        ua_ext[0:HALO_A, :] = jnp.zeros((HALO_A, D_A), F32)
        zb_ext[0:HALO_B, :] = jnp.zeros((HALO_B, D_B), F32)
        p_ext[0:HALO_C, :] = jnp.zeros((HALO_C, D_C), F32)

    ua_ext[HALO_A:HALO_A + tm, :] = ua_next[...]
    zb_ext[HALO_B:HALO_B + tm, :] = zb_next[...]
    p_ext[HALO_C:HALO_C + tm, :] = p_next[...]
    gb_cur[...] = gb_next[...]

    h_ref[...] = _rmsnorm(x_ref[...], g_ref[...]).astype(BF16)

    def put_block(lo, zc):
        if lo < OFF_AG:
            ua_next[:, lo:lo + Z_BLOCK] = zc + bin_ref[:, lo:lo + Z_BLOCK]
        elif lo < OFF_B:
            c = lo - OFF_AG
            ua_next[:, c:c + Z_BLOCK] = ua_next[:, c:c + Z_BLOCK] * jax.nn.sigmoid(zc + bin_ref[:, lo:lo + Z_BLOCK])
        elif lo < OFF_GB:
            zb_next[:, lo - OFF_B:lo - OFF_B + Z_BLOCK] = zc
        elif lo < OFF_GC:
            gb_next[:, lo - OFF_GB:lo - OFF_GB + Z_BLOCK] = zc
        elif lo < OFF_XV:
            p_next[:, lo - OFF_GC:lo - OFF_GC + Z_BLOCK] = zc
        else:
            c = lo - OFF_XV
            p_next[:, c:c + Z_BLOCK] = p_next[:, c:c + Z_BLOCK] * zc

    def project(lo):
        hi = min(lo + Z_CHUNK, D_IN)
        zc = jnp.dot(h_ref[...], win_ref[:, lo:hi], preferred_element_type=F32)
        for b in range(0, hi - lo, Z_BLOCK):
            put_block(lo + b, zc[:, b:b + Z_BLOCK])
        return zc[0:SUBLANES, 0:LANES]

    rows = CONV_GROUPS * SUBLANES
    row_in_vreg = lax.broadcasted_iota(jnp.int32, (SUBLANES, LANES), 0)
    never = (row_in_vreg + t) < 0

    def conv_block(base, c, tok):
        lanes = slice(c * LANES, (c + 1) * LANES)
        lo = base
        n_win = HALO_A // SUBLANES + CONV_GROUPS
        win = [jnp.where(never, tok, ua_ext[lo + SUBLANES * j:lo + SUBLANES * (j + 1), lanes]) for j in range(n_win)]
        acc = [jnp.zeros((SUBLANES, LANES), F32) for _ in range(CONV_GROUPS)]
        for r in range(SUBLANES):
            if r == 0:
                moved = win
            else:
                rolled = [pltpu.roll(v, r, 0) for v in win]
                moved = [None] + [jnp.where(row_in_vreg >= r, rolled[j], rolled[j - 1]) for j in range(1, n_win)]
            for q in range(HALO_A // SUBLANES):
                lag = SUBLANES * q + r
                if lag >= K_A:
                    continue
                wt = wdwa_ref[K_A - 1 - lag, :, lanes]
                for gi in range(CONV_GROUPS):
                    acc[gi] = acc[gi] + moved[HALO_A // SUBLANES + gi - q] * wt
        cv_buf[base:base + rows, lanes] = jnp.concatenate(acc, axis=0)

    def norm_rows(base, tok):
        va = cv_buf[base:base + rows, :] + bdwa_ref[...]
        mu = jnp.mean(va, axis=-1, keepdims=True)
        xc = va - mu
        var = jnp.mean(xc * xc, axis=-1, keepdims=True)
        y = xc * lax.rsqrt(var + EPS) * lng_ref[...] + lnb_ref[...]
        y = y * jax.nn.sigmoid(y)
        acts_ref[base:base + rows, 0:D_A] = y.astype(BF16)

    def pool_group(gi, tok):
        w = POOL_WINDOWS[gi]
        lo = gi * POOL_GROUP
        seen = (s_prev * tm + 1 + lax.broadcasted_iota(jnp.int32, (tm, 1), 0)).astype(F32)
        tok = zb_ext[HALO_B:HALO_B + tm, lo:lo + POOL_GROUP]
        win = tok
        for lag in range(1, w):
            win = win + zb_ext[HALO_B - lag:HALO_B - lag + tm, lo:lo + POOL_GROUP]
        pooled = win / jnp.minimum(seen, float(w)) - tok
        mixed = jnp.dot(pooled.astype(BF16), wpool_ref[gi], preferred_element_type=F32)
        acts_ref[:, ACT_B + lo:ACT_B + lo + POOL_GROUP] = (mixed * pscale_ref[:, lo:lo + POOL_GROUP]).astype(BF16)

    def short_conv(tok):
        cv = p_ext[HALO_C:HALO_C + tm, :] * wdwc_ref[K_C - 1:K_C, :]
        for lag in range(1, K_C):
            tap = K_C - 1 - lag
            cv = cv + p_ext[HALO_C - lag:HALO_C - lag + tm, :] * wdwc_ref[tap:tap + 1, :]
        acts_ref[:, ACT_C:ACT_C + D_C] = (gb_cur[...] * cv).astype(BF16)

    def keep_tails(tok):
        ua_ext[0:HALO_A, :] = ua_ext[tm:tm + HALO_A, :]
        zb_ext[0:HALO_B, :] = zb_ext[tm:tm + HALO_B, :]
        p_ext[0:HALO_C, :] = p_ext[tm:tm + HALO_C, :]

    mixer_work = []
    for base in range(0, tm, rows):
        mixer_work += [functools.partial(conv_block, base, c) for c in range(D_A // LANES)]
        mixer_work += [functools.partial(norm_rows, base)]
    mixer_work += ([functools.partial(pool_group, gi) for gi in range(len(POOL_WINDOWS))]
                  + [short_conv, keep_tails])
    chunks = list(range(0, D_IN, Z_CHUNK))
    per_chunk = -(-len(mixer_work) // (len(chunks) - 1))
    for ci, lo in enumerate(chunks):
        tok = project(lo)
        for work in mixer_work[ci * per_chunk:(ci + 1) * per_chunk]:
            work(tok)

    for src_ref, dst_ref in zip(cast_refs[:N_CAST], cast_refs[N_CAST:]):
        if len(dst_ref.shape) == 2:
            dst_ref[...] = src_ref[...].astype(BF16)
        else:
            n_blk, _, width = dst_ref.shape
            for k in range(n_blk):
                dst_ref[k] = src_ref[:, k * width:(k + 1) * width].astype(BF16)


def _in_mix(x, g, win, b_in, wdwa8, bdwa, lng, lnb, wpool, pscale, wdwc, cast_stacks, cast_layers, cast_widths, *,
            seq, tm):
    m, d = x.shape
    n_t = m // tm
    cur = lambda t: (jnp.minimum(t, n_t - 1), 0)
    cast_in, cast_out, cast_shape = [], [], []
    for w, layer, width in zip(cast_stacks, cast_layers, cast_widths):
        _, r, c = w.shape
        rps = max(r // n_t, BF16_ROWS)
        assert r % rps == 0 and r // rps <= n_t
        last = r // rps - 1
        cast_in.append(pl.BlockSpec((None, rps, c), lambda t, layer=layer, last=last: (layer, jnp.minimum(t, last), 0)))
        if width is None:
            cast_out.append(pl.BlockSpec((rps, c), lambda t, last=last: (jnp.minimum(t, last), 0)))
            cast_shape.append(jax.ShapeDtypeStruct((r, c), BF16))
        else:
            cast_out.append(pl.BlockSpec((c // width, rps, width), lambda t, last=last: (0, jnp.minimum(t, last), 0)))
            cast_shape.append(jax.ShapeDtypeStruct((c // width, r, width), BF16))
    prev = lambda t: (jnp.maximum(t - 1, 0), 0)
    const2 = lambda t: (0, 0)
    const3 = lambda t: (0, 0, 0)
    resident = functools.partial(pl.BlockSpec, pipeline_mode=pl.Buffered(1))
    return pl.pallas_call(
        functools.partial(_in_mix_kernel, tm=tm, tiles_per_seq=seq // tm),
        grid=(n_t + 1,),
        in_specs=[
            pl.BlockSpec((tm, d), cur),
            resident((1, d), const2),
            resident((d, D_IN), const2),
            resident((1, 2 * D_A), const2),
            resident((K_A, SUBLANES, D_A), const3),
            resident((1, D_A), const2),
            resident((1, D_A), const2),
            resident((1, D_A), const2),
            resident((len(POOL_WINDOWS), POOL_GROUP, POOL_GROUP), const3),
            resident((1, D_B), const2),
            resident((K_C, D_C), const2),
        ] + cast_in,
        out_specs=[pl.BlockSpec((tm, d), cur), pl.BlockSpec((tm, D_ACTS), prev)] + cast_out,
        out_shape=[jax.ShapeDtypeStruct((m, d), BF16), jax.ShapeDtypeStruct((m, D_ACTS), BF16)] + cast_shape,
        scratch_shapes=[
            pltpu.VMEM((tm, D_A), F32),
            pltpu.VMEM((tm, D_B), F32),
            pltpu.VMEM((tm, D_C), F32),
            pltpu.VMEM((tm, D_C), F32),
            pltpu.VMEM((tm, D_C), F32),
            pltpu.VMEM((HALO_A + tm, D_A), F32),
            pltpu.VMEM((tm, D_A), F32),
            pltpu.VMEM((HALO_B + tm, D_B), F32),
            pltpu.VMEM((HALO_C + tm, D_C), F32),
        ],
        compiler_params=_params("arbitrary"),
        name="in_mix",
    )(x, g, win, b_in, wdwa8, bdwa, lng, lnb, wpool, pscale, wdwc, *cast_stacks)


def _gate_merge_kernel(h_ref, acts_ref, x_ref, wg0_ref, wg1_ref, wg2_ref, bg0_ref, bg1_ref, bg2_ref,
                       wa_ref, wb_ref, wc_ref, wo_ref, gmlp_ref, o_ref, h2_ref):
    j = pl.program_id(1)
    h = h_ref[...]
    cb = wo_ref.shape[0]

    def merged_cols(cols):
        def gate(wg_ref, bg_ref):
            return jax.nn.sigmoid(jnp.dot(h, wg_ref[:, cols], preferred_element_type=F32) + bg_ref[:, cols])

        g0 = gate(wg0_ref, bg0_ref)
        g1 = gate(wg1_ref, bg1_ref)
        g2 = gate(wg2_ref, bg2_ref)
        mg = g0 * jnp.dot(acts_ref[:, 0:D_A], wa_ref[:, cols], preferred_element_type=F32)
        mg = mg + g1 * jnp.dot(acts_ref[:, ACT_B:ACT_B + D_B], wb_ref[:, cols], preferred_element_type=F32)
        mg = mg + g2 * jnp.dot(acts_ref[:, ACT_C:ACT_C + D_C], wc_ref[:, cols], preferred_element_type=F32)
        return mg.astype(BF16)

    @pl.when(j == 0)
    def _():
        o_ref[...] = x_ref[...]

    halves = [slice(0, cb // 2), slice(cb // 2, cb)]
    merged = [merged_cols(cols) for cols in halves]
    for cols, mg in zip(halves, merged):
        o_ref[...] += jnp.dot(mg, wo_ref[cols, :], preferred_element_type=F32)

    @pl.when(j == pl.num_programs(1) - 1)
    def _():
        h2_ref[...] = _rmsnorm(o_ref[...], gmlp_ref[...]).astype(BF16)


def _gate_merge(h, acts, x, wg, bg, wa, wb, wc, wo, gmlp, *, tm, cb):
    m, d = x.shape
    n_cb = d // cb
    tile = lambda i, j: (i, 0)
    blk = lambda i, j: (j, 0, 0)
    gate_blk = lambda k: (lambda i, j: (k * n_cb + j, 0, 0))
    gate_col = lambda k: (lambda i, j: (0, k * n_cb + j))
    assert wg.shape == (N_BRANCH * n_cb, d, cb) and wa.shape == (n_cb, D_A, cb)
    return pl.pallas_call(
        _gate_merge_kernel,
        grid=(m // tm, n_cb),
        in_specs=[
            pl.BlockSpec((tm, d), tile),
            pl.BlockSpec((tm, D_ACTS), tile),
            pl.BlockSpec((tm, d), tile),
            pl.BlockSpec((None, d, cb), gate_blk(0)),
            pl.BlockSpec((None, d, cb), gate_blk(1)),
            pl.BlockSpec((None, d, cb), gate_blk(2)),
            pl.BlockSpec((1, cb), gate_col(0)),
            pl.BlockSpec((1, cb), gate_col(1)),
            pl.BlockSpec((1, cb), gate_col(2)),
            pl.BlockSpec((None, D_A, cb), blk),
            pl.BlockSpec((None, D_B, cb), blk),
            pl.BlockSpec((None, D_C, cb), blk),
            pl.BlockSpec((cb, d), lambda i, j: (j, 0)),
            pl.BlockSpec((1, d), lambda i, j: (0, 0)),
        ],
        out_specs=[pl.BlockSpec((tm, d), tile), pl.BlockSpec((tm, d), tile)],
        out_shape=[jax.ShapeDtypeStruct((m, d), F32), jax.ShapeDtypeStruct((m, d), BF16)],
        compiler_params=_params("arbitrary", "arbitrary"),
        name="gate_merge",
    )(h, acts, x, wg, wg, wg, bg, bg, bg, wa, wb, wc, wo, gmlp)


def _mlp_kernel(h2_ref, x_ref, wu_ref, wd_ref, gf_ref, o_ref, *, final):
    f = pl.program_id(1)

    @pl.when(f == 0)
    def _():
        o_ref[...] = x_ref[...]

    a = jnp.dot(h2_ref[...], wu_ref[...], preferred_element_type=F32)
    a = jnp.square(jnp.maximum(a, 0.0)).astype(BF16)
    o_ref[...] += jnp.dot(a, wd_ref[...], preferred_element_type=F32)

    if final:
        @pl.when(f == pl.num_programs(1) - 1)
        def _():
            o_ref[...] = _rmsnorm(o_ref[...], gf_ref[...])


def _mlp(h2, x, wu, wd, gf, *, final, tm, tf):
    m, d = x.shape
    ff = wd.shape[0]
    assert wu.shape == (ff // tf, d, tf)
    tile = lambda i, f: (i, 0)
    return pl.pallas_call(
        functools.partial(_mlp_kernel, final=final),
        grid=(m // tm, ff // tf),
        in_specs=[
            pl.BlockSpec((tm, d), tile),
            pl.BlockSpec((tm, d), tile),
            pl.BlockSpec((None, d, tf), lambda i, f: (f, 0, 0)),
            pl.BlockSpec((tf, d), lambda i, f: (f, 0)),
            pl.BlockSpec((1, d), lambda i, f: (0, 0)),
        ],
        out_specs=pl.BlockSpec((tm, d), tile),
        out_shape=jax.ShapeDtypeStruct((m, d), F32),
        compiler_params=_params("arbitrary", "arbitrary"),
        name="mlp_final" if final else "mlp",
    )(h2, x, wu, wd, gf)


def kernel(x, g_mix, w_in, w_gate, b_gate, b_glu, w_dw_a, b_dw_a, ln_g_a, ln_b_a, w_a_out, w_pool_grp, pool_scale,
           w_b_out, w_dw_c, w_c_out, w_o, g_mlp, w_up, w_down, g_final):
    b, s, d = x.shape
    assert (d, w_in.shape[-1], w_up.shape[-1]) == (D_MODEL, D_IN, D_FF)
    xf = x.reshape(b * s, d)
    row = lambda v: v.reshape(1, -1)
    gf = row(g_final)
    win16 = w_in[0].astype(BF16)
    tm_mix, tm_merge, cb, tm_mlp, tf = 256, 512, 512, 1024, 512
    for l in range(DEPTH):
        wdwa8 = jnp.broadcast_to(w_dw_a[l][:, None, :], (K_A, SUBLANES, D_A))
        nxt = (l + 1) % DEPTH
        h, acts, wg16, wu16, wd16, wo16, win16_next, wa16, wb16, wc16 = _in_mix(
            xf, row(g_mix[l]), win16, row(b_glu[l]), wdwa8, row(b_dw_a[l]), row(ln_g_a[l]), row(ln_b_a[l]),
            w_pool_grp[l].astype(BF16), row(pool_scale[l]), w_dw_c[l],
            (w_gate, w_up, w_down, w_o, w_in, w_a_out, w_b_out, w_c_out), (l, l, l, l, nxt, l, l, l),
            (cb, tf, None, None, None, cb, cb, cb), seq=s, tm=tm_mix)
        xf, h2 = _gate_merge(h, acts, xf, wg16, row(b_gate[l]), wa16, wb16, wc16, wo16, row(g_mlp[l]),
                             tm=tm_merge, cb=cb)
        xf = _mlp(h2, xf, wu16, wd16, gf, final=(l == DEPTH - 1), tm=tm_mlp, tf=tf)
        win16 = win16_next
    return xf.reshape(b, s, d)
```

```python
import functools

import jax
import jax.numpy as jnp
from jax import lax
from jax.experimental import pallas as pl
from jax.experimental.pallas import tpu as pltpu

D_MODEL = 2048
DEPTH = 2
D_A = 768
K_A = 31
POOL_WINDOWS = (2, 4, 8, 16)
POOL_GROUP = 128
D_B = len(POOL_WINDOWS) * POOL_GROUP
D_C = 768
K_C = 3
D_IN = 2 * D_A + D_B + 3 * D_C
N_BRANCH = 3
D_FF = 4 * D_MODEL
EPS = 1e-6

OFF_AG = D_A
OFF_B = 2 * D_A
OFF_GB = OFF_B + D_B
OFF_GC = OFF_GB + D_C
OFF_XV = OFF_GC + D_C
D_ACTS = D_A + D_B + D_C
ACT_B = D_A
ACT_C = D_A + D_B

SUBLANES = 8
LANES = 128
HALO_A = 32
HALO_B = 16
HALO_C = 8
CONV_GROUPS = 4
Z_BLOCK = 256
Z_CHUNK = 2 * Z_BLOCK

V7X_VMEM_LIMIT = 60 * 1024 * 1024

BF16 = jnp.bfloat16
F32 = jnp.float32


def _rmsnorm(x, g):
    return x * lax.rsqrt(jnp.mean(x * x, axis=-1, keepdims=True) + EPS) * g


def _params(*sem, flags=None):
    return pltpu.CompilerParams(dimension_semantics=sem, vmem_limit_bytes=V7X_VMEM_LIMIT, flags=flags)


N_MIX_IN = 11
N_CAST = 5
BF16_ROWS = 16


def _in_mix_kernel(*refs, tm, tiles_per_seq, cast_groups):
    (x_ref, g_ref, win_ref, bin_ref, wdwa_ref, bdwa_ref, lng_ref, lnb_ref, wpool_ref, pscale_ref,
     wdwc_ref) = refs[:N_MIX_IN]
    h_ref, acts_ref = refs[N_MIX_IN + N_CAST:N_MIX_IN + N_CAST + 2]
    cast_refs = refs[N_MIX_IN:N_MIX_IN + N_CAST] + refs[N_MIX_IN + N_CAST + 2:N_MIX_IN + 2 * N_CAST + 2]
    ua_next, zb_next, gb_next, p_next, gb_cur, ua_ext, cv_buf, zb_ext, p_ext = refs[N_MIX_IN + 2 * N_CAST + 2:]
    t = pl.program_id(0)
    s_prev = lax.rem(t + tiles_per_seq - 1, tiles_per_seq)

    @pl.when(t == 0)
    def _():
        ua_next[...] = jnp.zeros_like(ua_next)
        zb_next[...] = jnp.zeros_like(zb_next)
        gb_next[...] = jnp.zeros_like(gb_next)
        p_next[...] = jnp.zeros_like(p_next)

    @pl.when(jnp.logical_or(t == 0, s_prev == 0))
    def _():
        ua_ext[0:HALO_A, :] = jnp.zeros((HALO_A, D_A), F32)
        zb_ext[0:HALO_B, :] = jnp.zeros((HALO_B, D_B), F32)
        p_ext[0:HALO_C, :] = jnp.zeros((HALO_C, D_C), F32)

    ua_ext[HALO_A:HALO_A + tm, :] = ua_next[...]
    zb_ext[HALO_B:HALO_B + tm, :] = zb_next[...]
    p_ext[HALO_C:HALO_C + tm, :] = p_next[...]
    gb_cur[...] = gb_next[...]

    h_ref[...] = _rmsnorm(x_ref[...], g_ref[...]).astype(BF16)

    def put_block(lo, zc):
        if lo < OFF_AG:
            ua_next[:, lo:lo + Z_BLOCK] = zc + bin_ref[:, lo:lo + Z_BLOCK]
        elif lo < OFF_B:
            c = lo - OFF_AG
            ua_next[:, c:c + Z_BLOCK] = ua_next[:, c:c + Z_BLOCK] * jax.nn.sigmoid(zc + bin_ref[:, lo:lo + Z_BLOCK])
        elif lo < OFF_GB:
            zb_next[:, lo - OFF_B:lo - OFF_B + Z_BLOCK] = zc
        elif lo < OFF_GC:
            gb_next[:, lo - OFF_GB:lo - OFF_GB + Z_BLOCK] = zc
        elif lo < OFF_XV:
            p_next[:, lo - OFF_GC:lo - OFF_GC + Z_BLOCK] = zc
        else:
            c = lo - OFF_XV
            p_next[:, c:c + Z_BLOCK] = p_next[:, c:c + Z_BLOCK] * zc

    def project(lo):
        hi = min(lo + Z_CHUNK, D_IN)
        zc = jnp.dot(h_ref[...], win_ref[:, lo:hi], preferred_element_type=F32)
        for b in range(0, hi - lo, Z_BLOCK):
            put_block(lo + b, zc[:, b:b + Z_BLOCK])
        return zc[0:SUBLANES, 0:LANES]

    rows = CONV_GROUPS * SUBLANES
    row_in_vreg = lax.broadcasted_iota(jnp.int32, (SUBLANES, LANES), 0)
    never = (row_in_vreg + t) < 0

    def conv_block(base, c, tok):
        lanes = slice(c * LANES, (c + 1) * LANES)
        lo = base
        n_win = HALO_A // SUBLANES + CONV_GROUPS
        win = [jnp.where(never, tok, ua_ext[lo + SUBLANES * j:lo + SUBLANES * (j + 1), lanes]) for j in range(n_win)]
        acc = [jnp.zeros((SUBLANES, LANES), F32) for _ in range(CONV_GROUPS)]
        for r in range(SUBLANES):
            if r == 0:
                moved = win
            else:
                rolled = [pltpu.roll(v, r, 0) for v in win]
                moved = [None] + [jnp.where(row_in_vreg >= r, rolled[j], rolled[j - 1]) for j in range(1, n_win)]
            for q in range(HALO_A // SUBLANES):
                lag = SUBLANES * q + r
                if lag >= K_A:
                    continue
                wt = wdwa_ref[K_A - 1 - lag, :, lanes]
                for gi in range(CONV_GROUPS):
                    acc[gi] = acc[gi] + moved[HALO_A // SUBLANES + gi - q] * wt
        cv_buf[base:base + rows, lanes] = jnp.concatenate(acc, axis=0)

    def norm_rows(base, tok):
        va = cv_buf[base:base + rows, :] + bdwa_ref[...]
        mu = jnp.mean(va, axis=-1, keepdims=True)
        xc = va - mu
        var = jnp.mean(xc * xc, axis=-1, keepdims=True)
        y = xc * lax.rsqrt(var + EPS) * lng_ref[...] + lnb_ref[...]
        y = y * jax.nn.sigmoid(y)
        acts_ref[base:base + rows, 0:D_A] = y.astype(BF16)

    def pool_group(gi, tok):
        w = POOL_WINDOWS[gi]
        lo = gi * POOL_GROUP
        seen = (s_prev * tm + 1 + lax.broadcasted_iota(jnp.int32, (tm, 1), 0)).astype(F32)
        tok = zb_ext[HALO_B:HALO_B + tm, lo:lo + POOL_GROUP]
        win = tok
        for lag in range(1, w):
            win = win + zb_ext[HALO_B - lag:HALO_B - lag + tm, lo:lo + POOL_GROUP]
        pooled = win / jnp.minimum(seen, float(w)) - tok
        mixed = jnp.dot(pooled.astype(BF16), wpool_ref[gi], preferred_element_type=F32)
        acts_ref[:, ACT_B + lo:ACT_B + lo + POOL_GROUP] = (mixed * pscale_ref[:, lo:lo + POOL_GROUP]).astype(BF16)

    def short_conv(tok):
        cv = p_ext[HALO_C:HALO_C + tm, :] * wdwc_ref[K_C - 1:K_C, :]
        for lag in range(1, K_C):
            tap = K_C - 1 - lag
            cv = cv + p_ext[HALO_C - lag:HALO_C - lag + tm, :] * wdwc_ref[tap:tap + 1, :]
        acts_ref[:, ACT_C:ACT_C + D_C] = (gb_cur[...] * cv).astype(BF16)

    def keep_tails(tok):
        ua_ext[0:HALO_A, :] = ua_ext[tm:tm + HALO_A, :]
        zb_ext[0:HALO_B, :] = zb_ext[tm:tm + HALO_B, :]
        p_ext[0:HALO_C, :] = p_ext[tm:tm + HALO_C, :]

    mixer_work = []
    for base in range(0, tm, rows):
        mixer_work += [functools.partial(conv_block, base, c) for c in range(D_A // LANES)]
        mixer_work += [functools.partial(norm_rows, base)]
    mixer_work += ([functools.partial(pool_group, gi) for gi in range(len(POOL_WINDOWS))]
                  + [short_conv, keep_tails])
    chunks = list(range(0, D_IN, Z_CHUNK))
    per_chunk = -(-len(mixer_work) // (len(chunks) - 1))
    for ci, lo in enumerate(chunks):
        tok = project(lo)
        for work in mixer_work[ci * per_chunk:(ci + 1) * per_chunk]:
            work(tok)

    for src_ref, dst_ref, groups in zip(cast_refs[:N_CAST], cast_refs[N_CAST:], cast_groups):
        if groups is None:
            dst_ref[...] = src_ref[...].astype(BF16)
        else:
            n_blk, _, width = dst_ref.shape
            sub = width // groups
            for j in range(n_blk):
                for k in range(groups):
                    col = (k * n_blk + j) * sub
                    dst_ref[j, :, k * sub:(k + 1) * sub] = src_ref[:, col:col + sub].astype(BF16)


def _in_mix(x, g, win, b_in, wdwa8, bdwa, lng, lnb, wpool, pscale, wdwc, cast_stacks, cast_layers, cast_widths, *,
            seq, tm):
    m, d = x.shape
    n_t = m // tm
    cur = lambda t: (jnp.minimum(t, n_t - 1), 0)
    cast_in, cast_out, cast_shape = [], [], []
    cast_groups = tuple(None if wg is None else wg[1] for wg in cast_widths)
    for w, layer, width in zip(cast_stacks, cast_layers, (None if wg is None else wg[0] for wg in cast_widths)):
        _, r, c = w.shape
        rps = max(r // n_t, BF16_ROWS)
        assert r % rps == 0 and r // rps <= n_t
        last = r // rps - 1
        cast_in.append(pl.BlockSpec((None, rps, c), lambda t, layer=layer, last=last: (layer, jnp.minimum(t, last), 0)))
        if width is None:
            cast_out.append(pl.BlockSpec((rps, c), lambda t, last=last: (jnp.minimum(t, last), 0)))
            cast_shape.append(jax.ShapeDtypeStruct((r, c), BF16))
        else:
            cast_out.append(pl.BlockSpec((c // width, rps, width), lambda t, last=last: (0, jnp.minimum(t, last), 0)))
            cast_shape.append(jax.ShapeDtypeStruct((c // width, r, width), BF16))
    prev = lambda t: (jnp.maximum(t - 1, 0), 0)
    const2 = lambda t: (0, 0)
    const3 = lambda t: (0, 0, 0)
    resident = functools.partial(pl.BlockSpec, pipeline_mode=pl.Buffered(1))
    return pl.pallas_call(
        functools.partial(_in_mix_kernel, tm=tm, tiles_per_seq=seq // tm, cast_groups=cast_groups),
        grid=(n_t + 1,),
        in_specs=[
            pl.BlockSpec((tm, d), cur),
            resident((1, d), const2),
            resident((d, D_IN), const2),
            resident((1, 2 * D_A), const2),
            resident((K_A, SUBLANES, D_A), const3),
            resident((1, D_A), const2),
            resident((1, D_A), const2),
            resident((1, D_A), const2),
            resident((len(POOL_WINDOWS), POOL_GROUP, POOL_GROUP), const3),
            resident((1, D_B), const2),
            resident((K_C, D_C), const2),
        ] + cast_in,
        out_specs=[pl.BlockSpec((tm, d), cur), pl.BlockSpec((tm, D_ACTS), prev)] + cast_out,
        out_shape=[jax.ShapeDtypeStruct((m, d), BF16), jax.ShapeDtypeStruct((m, D_ACTS), BF16)] + cast_shape,
        scratch_shapes=[
            pltpu.VMEM((tm, D_A), F32),
            pltpu.VMEM((tm, D_B), F32),
            pltpu.VMEM((tm, D_C), F32),
            pltpu.VMEM((tm, D_C), F32),
            pltpu.VMEM((tm, D_C), F32),
            pltpu.VMEM((HALO_A + tm, D_A), F32),
            pltpu.VMEM((tm, D_A), F32),
            pltpu.VMEM((HALO_B + tm, D_B), F32),
            pltpu.VMEM((HALO_C + tm, D_C), F32),
        ],
        compiler_params=_params("arbitrary"),
        name="in_mix",
    )(x, g, win, b_in, wdwa8, bdwa, lng, lnb, wpool, pscale, wdwc, *cast_stacks)


def _gate_merge_kernel(h_ref, acts_ref, x_ref, wg_ref, bg_ref, wabc_ref, wo_ref, gmlp_ref, o_ref, h2_ref):
    j = pl.program_id(1)
    h = h_ref[...]
    cb = wo_ref.shape[0]
    bg = bg_ref[j]

    def merged_cols(lo):
        def gate(k):
            cols = slice(k * cb + lo, k * cb + lo + cb // 2)
            return jax.nn.sigmoid(jnp.dot(h, wg_ref[:, cols], preferred_element_type=F32) + bg[:, cols])

        def branch(rows):
            return jnp.dot(acts_ref[:, rows], wabc_ref[rows, lo:lo + cb // 2], preferred_element_type=F32)

        g0, g1, g2 = gate(0), gate(1), gate(2)
        mg = g0 * branch(slice(0, D_A))
        mg = mg + g1 * branch(slice(ACT_B, ACT_B + D_B))
        mg = mg + g2 * branch(slice(ACT_C, ACT_C + D_C))
        return mg.astype(BF16)

    @pl.when(j == 0)
    def _():
        o_ref[...] = x_ref[...]

    halves = [0, cb // 2]
    merged = [merged_cols(lo) for lo in halves]
    for lo, mg in zip(halves, merged):
        o_ref[...] += jnp.dot(mg, wo_ref[lo:lo + cb // 2, :], preferred_element_type=F32)

    @pl.when(j == pl.num_programs(1) - 1)
    def _():
        h2_ref[...] = _rmsnorm(o_ref[...], gmlp_ref[...]).astype(BF16)


def _gate_merge(h, acts, x, wg, bg, wabc, wo, gmlp, *, tm, cb):
    m, d = x.shape
    n_cb = d // cb
    tile = lambda i, j: (i, 0)
    blk = lambda i, j: (j, 0, 0)
    assert wg.shape == (n_cb, d, N_BRANCH * cb) and bg.shape == (n_cb, 1, N_BRANCH * cb)
    assert wabc.shape == (n_cb, D_ACTS, cb)
    return pl.pallas_call(
        _gate_merge_kernel,
        grid=(m // tm, n_cb),
        in_specs=[
            pl.BlockSpec((tm, d), tile),
            pl.BlockSpec((tm, D_ACTS), tile),
            pl.BlockSpec((tm, d), tile),
            pl.BlockSpec((None, d, N_BRANCH * cb), blk),
            pl.BlockSpec((n_cb, 1, N_BRANCH * cb), lambda i, j: (0, 0, 0)),
            pl.BlockSpec((None, D_ACTS, cb), blk),
            pl.BlockSpec((cb, d), lambda i, j: (j, 0)),
            pl.BlockSpec((1, d), lambda i, j: (0, 0)),
        ],
        out_specs=[pl.BlockSpec((tm, d), tile), pl.BlockSpec((tm, d), tile)],
        out_shape=[jax.ShapeDtypeStruct((m, d), F32), jax.ShapeDtypeStruct((m, d), BF16)],
        compiler_params=_params("arbitrary", "arbitrary"),
        name="gate_merge",
    )(h, acts, x, wg, bg, wabc, wo, gmlp)


def _mlp_kernel(h2_ref, x_ref, wu_ref, wd_ref, gf_ref, o_ref, *, final):
    f = pl.program_id(1)

    @pl.when(f == 0)
    def _():
        o_ref[...] = x_ref[...]

    a = jnp.dot(h2_ref[...], wu_ref[...], preferred_element_type=F32)
    a = jnp.square(jnp.maximum(a, 0.0)).astype(BF16)
    o_ref[...] += jnp.dot(a, wd_ref[...], preferred_element_type=F32)

    if final:
        @pl.when(f == pl.num_programs(1) - 1)
        def _():
            o_ref[...] = _rmsnorm(o_ref[...], gf_ref[...])


def _mlp(h2, x, wu, wd, gf, *, final, tm, tf):
    m, d = x.shape
    ff = wd.shape[0]
    assert wu.shape == (ff // tf, d, tf)
    tile = lambda i, f: (i, 0)
    return pl.pallas_call(
        functools.partial(_mlp_kernel, final=final),
        grid=(m // tm, ff // tf),
        in_specs=[
            pl.BlockSpec((tm, d), tile),
            pl.BlockSpec((tm, d), tile),
            pl.BlockSpec((None, d, tf), lambda i, f: (f, 0, 0)),
            pl.BlockSpec((tf, d), lambda i, f: (f, 0)),
            pl.BlockSpec((1, d), lambda i, f: (0, 0)),
        ],
        out_specs=pl.BlockSpec((tm, d), tile),
        out_shape=jax.ShapeDtypeStruct((m, d), F32),
        compiler_params=_params("arbitrary", "arbitrary"),
        name="mlp_final" if final else "mlp",
    )(h2, x, wu, wd, gf)


def kernel(x, g_mix, w_in, w_gate, b_gate, b_glu, w_dw_a, b_dw_a, ln_g_a, ln_b_a, w_a_out, w_pool_grp, pool_scale,
           w_b_out, w_dw_c, w_c_out, w_o, g_mlp, w_up, w_down, g_final):
    b, s, d = x.shape
    assert (d, w_in.shape[-1], w_up.shape[-1]) == (D_MODEL, D_IN, D_FF)
    xf = x.reshape(b * s, d)
    row = lambda v: v.reshape(1, -1)
    gf = row(g_final)
    win16 = w_in[0].astype(BF16)
    tm_mix, tm_merge, cb, tm_mlp, tf = 256, 512, 512, 1024, 512
    col_blocks = lambda w: w.astype(BF16).reshape(w.shape[0], -1, cb).transpose(1, 0, 2)
    for l in range(DEPTH):
        wdwa8 = jnp.broadcast_to(w_dw_a[l][:, None, :], (K_A, SUBLANES, D_A))
        nxt = (l + 1) % DEPTH
        h, acts, wg16, wu16, wd16, wo16, win16_next = _in_mix(
            xf, row(g_mix[l]), win16, row(b_glu[l]), wdwa8, row(b_dw_a[l]), row(ln_g_a[l]), row(ln_b_a[l]),
            w_pool_grp[l].astype(BF16), row(pool_scale[l]), w_dw_c[l],
            (w_gate, w_up, w_down, w_o, w_in), (l, l, l, l, nxt),
            ((N_BRANCH * cb, N_BRANCH), (tf, 1), None, None, None), seq=s, tm=tm_mix)
        bg = b_gate[l].reshape(N_BRANCH, d // cb, 1, cb).transpose(1, 2, 0, 3).reshape(d // cb, 1, N_BRANCH * cb)
        wabc = col_blocks(jnp.concatenate([w_a_out[l], w_b_out[l], w_c_out[l]], axis=0))
        xf, h2 = _gate_merge(h, acts, xf, wg16, bg, wabc, wo16, row(g_mlp[l]), tm=tm_merge, cb=cb)
        xf = _mlp(h2, xf, wu16, wd16, gf, final=(l == DEPTH - 1), tm=tm_mlp, tf=tf)
        win16 = win16_next
    return xf.reshape(b, s, d)
```

```python
import functools

import jax
import jax.numpy as jnp
from jax import lax
from jax.experimental import pallas as pl
from jax.experimental.pallas import tpu as pltpu

D_MODEL = 2048
DEPTH = 2
D_A = 768
K_A = 31
POOL_WINDOWS = (2, 4, 8, 16)
POOL_GROUP = 128
D_B = len(POOL_WINDOWS) * POOL_GROUP
D_C = 768
K_C = 3
D_IN = 2 * D_A + D_B + 3 * D_C
N_BRANCH = 3
D_FF = 4 * D_MODEL
EPS = 1e-6

OFF_AG = D_A
OFF_B = 2 * D_A
OFF_GB = OFF_B + D_B
OFF_GC = OFF_GB + D_C
OFF_XV = OFF_GC + D_C
D_ACTS = D_A + D_B + D_C
ACT_B = D_A
ACT_C = D_A + D_B

SUBLANES = 8
LANES = 128
HALO_A = 32
HALO_B = 16
HALO_C = 8
CONV_GROUPS = 4
Z_BLOCK = 256
Z_CHUNK = 2 * Z_BLOCK

V7X_VMEM_LIMIT = 60 * 1024 * 1024

BF16 = jnp.bfloat16
F32 = jnp.float32


def _rmsnorm(x, g):
    return x * lax.rsqrt(jnp.mean(x * x, axis=-1, keepdims=True) + EPS) * g


def _params(*sem, flags=None):
    return pltpu.CompilerParams(dimension_semantics=sem, vmem_limit_bytes=V7X_VMEM_LIMIT, flags=flags)


N_MIX_IN = 11
N_CAST = 5


def _in_mix_kernel(*refs, tm, tiles_per_seq):
    (x_ref, g_ref, win_ref, bin_ref, wdwa_ref, bdwa_ref, lng_ref, lnb_ref, wpool_ref, pscale_ref,
     wdwc_ref) = refs[:N_MIX_IN]
    h_ref, acts_ref = refs[N_MIX_IN + N_CAST:N_MIX_IN + N_CAST + 2]
    cast_refs = refs[N_MIX_IN:N_MIX_IN + N_CAST] + refs[N_MIX_IN + N_CAST + 2:N_MIX_IN + 2 * N_CAST + 2]
    gb_cur, ua_ext, cv_buf, zb_ext, p_ext = refs[N_MIX_IN + 2 * N_CAST + 2:]
    t = pl.program_id(0)
    s_prev = lax.rem(t, tiles_per_seq)

    @pl.when(s_prev == 0)
    def _():
        ua_ext[0:HALO_A, :] = jnp.zeros((HALO_A, D_A), F32)
        zb_ext[0:HALO_B, :] = jnp.zeros((HALO_B, D_B), F32)
        p_ext[0:HALO_C, :] = jnp.zeros((HALO_C, D_C), F32)

    h_ref[...] = _rmsnorm(x_ref[...], g_ref[...]).astype(BF16)

    cur_a, cur_b, cur_c = slice(HALO_A, HALO_A + tm), slice(HALO_B, HALO_B + tm), slice(HALO_C, HALO_C + tm)

    def put_block(lo, zc):
        if lo < OFF_AG:
            ua_ext[cur_a, lo:lo + Z_BLOCK] = zc
        elif lo < OFF_B:
            c = lo - OFF_AG
            ua_ext[cur_a, c:c + Z_BLOCK] = ua_ext[cur_a, c:c + Z_BLOCK] * jax.nn.sigmoid(zc)
        elif lo < OFF_GB:
            zb_ext[cur_b, lo - OFF_B:lo - OFF_B + Z_BLOCK] = zc
        elif lo < OFF_GC:
            gb_cur[:, lo - OFF_GB:lo - OFF_GB + Z_BLOCK] = zc
        elif lo < OFF_XV:
            p_ext[cur_c, lo - OFF_GC:lo - OFF_GC + Z_BLOCK] = zc
        else:
            c = lo - OFF_XV
            p_ext[cur_c, c:c + Z_BLOCK] = p_ext[cur_c, c:c + Z_BLOCK] * zc

    def project(lo):
        hi = min(lo + Z_CHUNK, D_IN)
        zc = jnp.dot(h_ref[...], win_ref[:, lo:hi], preferred_element_type=F32) + bin_ref[:, lo:hi]
        for b in range(0, hi - lo, Z_BLOCK):
            put_block(lo + b, zc[:, b:b + Z_BLOCK])
        return zc[0:SUBLANES, 0:LANES]

    rows = CONV_GROUPS * SUBLANES
    row_in_vreg = lax.broadcasted_iota(jnp.int32, (SUBLANES, LANES), 0)
    never = (row_in_vreg + t) < 0

    def conv_block(base, c, tok):
        lanes = slice(c * LANES, (c + 1) * LANES)
        lo = base
        n_win = HALO_A // SUBLANES + CONV_GROUPS
        win = [jnp.where(never, tok, ua_ext[lo + SUBLANES * j:lo + SUBLANES * (j + 1), lanes]) for j in range(n_win)]
        acc = [jnp.zeros((SUBLANES, LANES), F32) for _ in range(CONV_GROUPS)]
        for r in range(SUBLANES):
            if r == 0:
                moved = win
            else:
                rolled = [pltpu.roll(v, r, 0) for v in win]
                moved = [None] + [jnp.where(row_in_vreg >= r, rolled[j], rolled[j - 1]) for j in range(1, n_win)]
            for q in range(HALO_A // SUBLANES):
                lag = SUBLANES * q + r
                if lag >= K_A:
                    continue
                wt = wdwa_ref[K_A - 1 - lag, :, lanes]
                for gi in range(CONV_GROUPS):
                    acc[gi] = acc[gi] + moved[HALO_A // SUBLANES + gi - q] * wt
        cv_buf[base:base + rows, lanes] = jnp.concatenate(acc, axis=0)

    def norm_rows(base, tok):
        va = cv_buf[base:base + rows, :] + bdwa_ref[...]
        mu = jnp.mean(va, axis=-1, keepdims=True)
        xc = va - mu
        var = jnp.mean(xc * xc, axis=-1, keepdims=True)
        y = xc * lax.rsqrt(var + EPS) * lng_ref[...] + lnb_ref[...]
        y = y * jax.nn.sigmoid(y)
        acts_ref[base:base + rows, 0:D_A] = y.astype(BF16)

    def pool_group(gi, tok):
        w = POOL_WINDOWS[gi]
        lo = gi * POOL_GROUP
        seen = (s_prev * tm + 1 + lax.broadcasted_iota(jnp.int32, (tm, 1), 0)).astype(F32)
        tok = zb_ext[HALO_B:HALO_B + tm, lo:lo + POOL_GROUP]
        win = tok
        for lag in range(1, w):
            win = win + zb_ext[HALO_B - lag:HALO_B - lag + tm, lo:lo + POOL_GROUP]
        pooled = win / jnp.minimum(seen, float(w)) - tok
        mixed = jnp.dot(pooled.astype(BF16), wpool_ref[gi], preferred_element_type=F32)
        acts_ref[:, ACT_B + lo:ACT_B + lo + POOL_GROUP] = (mixed * pscale_ref[:, lo:lo + POOL_GROUP]).astype(BF16)

    def short_conv(tok):
        cv = p_ext[HALO_C:HALO_C + tm, :] * wdwc_ref[K_C - 1:K_C, :]
        for lag in range(1, K_C):
            tap = K_C - 1 - lag
            cv = cv + p_ext[HALO_C - lag:HALO_C - lag + tm, :] * wdwc_ref[tap:tap + 1, :]
        acts_ref[:, ACT_C:ACT_C + D_C] = (gb_cur[...] * cv).astype(BF16)

    def keep_tails(tok):
        ua_ext[0:HALO_A, :] = ua_ext[tm:tm + HALO_A, :]
        zb_ext[0:HALO_B, :] = zb_ext[tm:tm + HALO_B, :]
        p_ext[0:HALO_C, :] = p_ext[tm:tm + HALO_C, :]

    chunks = list(range(0, D_IN, Z_CHUNK))
    first = 2 * D_A // Z_CHUNK - 1
    early = (OFF_AG + Z_CHUNK) // Z_CHUNK * Z_CHUNK - OFF_AG
    early_work, conv_work = [], []
    for base in range(0, tm, rows):
        early_work += [functools.partial(conv_block, base, c) for c in range(early // LANES)]
        conv_work += [functools.partial(conv_block, base, c) for c in range(early // LANES, D_A // LANES)]
        conv_work += [functools.partial(norm_rows, base)]
    last = len(chunks) - 2
    per_chunk = -(-len(conv_work) // (last - first))
    for ci, lo in enumerate(chunks):
        tok = project(lo)
        if ci == first - 1:
            for work in early_work:
                work(tok)
        if first <= ci < last:
            for work in conv_work[(ci - first) * per_chunk:(ci - first + 1) * per_chunk]:
                work(tok)
        if ci == last - 1:
            for gi in range(len(POOL_WINDOWS)):
                pool_group(gi, tok)
    short_conv(tok)
    keep_tails(tok)

    for src_ref, dst_ref in zip(cast_refs[:N_CAST], cast_refs[N_CAST:]):
        if len(dst_ref.shape) == 2:
            dst_ref[...] = src_ref[...].astype(BF16)
        else:
            n_blk, _, width = dst_ref.shape
            for k in range(n_blk):
                dst_ref[k] = src_ref[:, k * width:(k + 1) * width].astype(BF16)


def _in_mix(x, g, win, b_in, wdwa8, bdwa, lng, lnb, wpool, pscale, wdwc, cast_stacks, cast_layers, cast_widths, *,
            seq, tm):
    m, d = x.shape
    n_t = m // tm
    cur = lambda t: (jnp.minimum(t, n_t - 1), 0)
    cast_in, cast_out, cast_shape = [], [], []
    for w, layer, width in zip(cast_stacks, cast_layers, cast_widths):
        _, r, c = w.shape
        assert r % n_t == 0
        cast_in.append(pl.BlockSpec((None, r // n_t, c), lambda t, layer=layer: (layer, jnp.minimum(t, n_t - 1), 0)))
        if width is None:
            cast_out.append(pl.BlockSpec((r // n_t, c), cur))
            cast_shape.append(jax.ShapeDtypeStruct((r, c), BF16))
        else:
            cast_out.append(pl.BlockSpec((c // width, r // n_t, width), lambda t: (0, jnp.minimum(t, n_t - 1), 0)))
            cast_shape.append(jax.ShapeDtypeStruct((c // width, r, width), BF16))
    const2 = lambda t: (0, 0)
    const3 = lambda t: (0, 0, 0)
    resident = functools.partial(pl.BlockSpec, pipeline_mode=pl.Buffered(1))
    return pl.pallas_call(
        functools.partial(_in_mix_kernel, tm=tm, tiles_per_seq=seq // tm),
        grid=(n_t,),
        in_specs=[
            pl.BlockSpec((tm, d), cur),
            resident((1, d), const2),
            resident((d, D_IN), const2),
            resident((1, D_IN), const2),
            resident((K_A, SUBLANES, D_A), const3),
            resident((1, D_A), const2),
            resident((1, D_A), const2),
            resident((1, D_A), const2),
            resident((len(POOL_WINDOWS), POOL_GROUP, POOL_GROUP), const3),
            resident((1, D_B), const2),
            resident((K_C, D_C), const2),
        ] + cast_in,
        out_specs=[pl.BlockSpec((tm, d), cur), pl.BlockSpec((tm, D_ACTS), cur)] + cast_out,
        out_shape=[jax.ShapeDtypeStruct((m, d), BF16), jax.ShapeDtypeStruct((m, D_ACTS), BF16)] + cast_shape,
        scratch_shapes=[
            pltpu.VMEM((tm, D_C), F32),
            pltpu.VMEM((HALO_A + tm, D_A), F32),
            pltpu.VMEM((tm, D_A), F32),
            pltpu.VMEM((HALO_B + tm, D_B), F32),
            pltpu.VMEM((HALO_C + tm, D_C), F32),
        ],
        compiler_params=_params("arbitrary"),
        name="in_mix",
    )(x, g, win, b_in, wdwa8, bdwa, lng, lnb, wpool, pscale, wdwc, *cast_stacks)


def _gate_merge_kernel(h_ref, acts_ref, x_ref, wg0_ref, wg1_ref, wg2_ref, bg0_ref, bg1_ref, bg2_ref,
                       wa_ref, wb_ref, wc_ref, wo_ref, gmlp_ref, o_ref, h2_ref):
    j = pl.program_id(1)
    h = h_ref[...]
    cb = wo_ref.shape[0]

    def merged_cols(cols):
        def gate(wg_ref, bg_ref):
            return jax.nn.sigmoid(jnp.dot(h, wg_ref[:, cols], preferred_element_type=F32) + bg_ref[:, cols])

        g0 = gate(wg0_ref, bg0_ref)
        g1 = gate(wg1_ref, bg1_ref)
        g2 = gate(wg2_ref, bg2_ref)
        mg = g0 * jnp.dot(acts_ref[:, 0:D_A], wa_ref[:, cols], preferred_element_type=F32)
        mg = mg + g1 * jnp.dot(acts_ref[:, ACT_B:ACT_B + D_B], wb_ref[:, cols], preferred_element_type=F32)
        mg = mg + g2 * jnp.dot(acts_ref[:, ACT_C:ACT_C + D_C], wc_ref[:, cols], preferred_element_type=F32)
        return mg.astype(BF16)

    @pl.when(j == 0)
    def _():
        o_ref[...] = x_ref[...]

    halves = [slice(0, cb // 2), slice(cb // 2, cb)]
    merged = [merged_cols(cols) for cols in halves]
    for cols, mg in zip(halves, merged):
        o_ref[...] += jnp.dot(mg, wo_ref[cols, :], preferred_element_type=F32)

    @pl.when(j == pl.num_programs(1) - 1)
    def _():
        h2_ref[...] = _rmsnorm(o_ref[...], gmlp_ref[...]).astype(BF16)


def _gate_merge(h, acts, x, wg, bg, wa, wb, wc, wo, gmlp, *, tm, cb):
    m, d = x.shape
    n_cb = d // cb
    tile = lambda i, j: (i, 0)
    blk = lambda i, j: (j, 0, 0)
    gate_blk = lambda k: (lambda i, j: (k * n_cb + j, 0, 0))
    gate_col = lambda k: (lambda i, j: (0, k * n_cb + j))
    assert wg.shape == (N_BRANCH * n_cb, d, cb) and wa.shape == (n_cb, D_A, cb)
    return pl.pallas_call(
        _gate_merge_kernel,
        grid=(m // tm, n_cb),
        in_specs=[
            pl.BlockSpec((tm, d), tile),
            pl.BlockSpec((tm, D_ACTS), tile),
            pl.BlockSpec((tm, d), tile),
            pl.BlockSpec((None, d, cb), gate_blk(0)),
            pl.BlockSpec((None, d, cb), gate_blk(1)),
            pl.BlockSpec((None, d, cb), gate_blk(2)),
            pl.BlockSpec((1, cb), gate_col(0)),
            pl.BlockSpec((1, cb), gate_col(1)),
            pl.BlockSpec((1, cb), gate_col(2)),
            pl.BlockSpec((None, D_A, cb), blk),
            pl.BlockSpec((None, D_B, cb), blk),
            pl.BlockSpec((None, D_C, cb), blk),
            pl.BlockSpec((cb, d), lambda i, j: (j, 0)),
            pl.BlockSpec((1, d), lambda i, j: (0, 0)),
        ],
        out_specs=[pl.BlockSpec((tm, d), tile), pl.BlockSpec((tm, d), tile)],
        out_shape=[jax.ShapeDtypeStruct((m, d), F32), jax.ShapeDtypeStruct((m, d), BF16)],
        compiler_params=_params("arbitrary", "arbitrary"),
        name="gate_merge",
    )(h, acts, x, wg, wg, wg, bg, bg, bg, wa, wb, wc, wo, gmlp)


def _mlp_kernel(h2_ref, x_ref, wu_ref, wd_ref, gf_ref, o_ref, *, final):
    f = pl.program_id(1)

    @pl.when(f == 0)
    def _():
        o_ref[...] = x_ref[...]

    a = jnp.dot(h2_ref[...], wu_ref[...], preferred_element_type=F32)
    a = jnp.square(jnp.maximum(a, 0.0)).astype(BF16)
    o_ref[...] += jnp.dot(a, wd_ref[...], preferred_element_type=F32)

    if final:
        @pl.when(f == pl.num_programs(1) - 1)
        def _():
            o_ref[...] = _rmsnorm(o_ref[...], gf_ref[...])


def _mlp(h2, x, wu, wd, gf, *, final, tm, tf):
    m, d = x.shape
    ff = wd.shape[0]
    assert wu.shape == (ff // tf, d, tf)
    tile = lambda i, f: (i, 0)
    return pl.pallas_call(
        functools.partial(_mlp_kernel, final=final),
        grid=(m // tm, ff // tf),
        in_specs=[
            pl.BlockSpec((tm, d), tile),
            pl.BlockSpec((tm, d), tile),
            pl.BlockSpec((None, d, tf), lambda i, f: (f, 0, 0)),
            pl.BlockSpec((tf, d), lambda i, f: (f, 0)),
            pl.BlockSpec((1, d), lambda i, f: (0, 0)),
        ],
        out_specs=pl.BlockSpec((tm, d), tile),
        out_shape=jax.ShapeDtypeStruct((m, d), F32),
        compiler_params=_params("arbitrary", "arbitrary"),
        name="mlp_final" if final else "mlp",
    )(h2, x, wu, wd, gf)


def kernel(x, g_mix, w_in, w_gate, b_gate, b_glu, w_dw_a, b_dw_a, ln_g_a, ln_b_a, w_a_out, w_pool_grp, pool_scale,
           w_b_out, w_dw_c, w_c_out, w_o, g_mlp, w_up, w_down, g_final):
    b, s, d = x.shape
    assert (d, w_in.shape[-1], w_up.shape[-1]) == (D_MODEL, D_IN, D_FF)
    xf = x.reshape(b * s, d)
    row = lambda v: v.reshape(1, -1)
    gf = row(g_final)
    win16 = w_in[0].astype(BF16)
    tm_mix, tm_merge, cb, tm_mlp, tf = 256, 512, 512, 1024, 512
    col_blocks = lambda w: w.astype(BF16).reshape(w.shape[0], -1, cb).transpose(1, 0, 2)
    for l in range(DEPTH):
        b_in = jnp.concatenate([b_glu[l], jnp.zeros((D_IN - 2 * D_A,), F32)]).reshape(1, D_IN)
        wdwa8 = jnp.broadcast_to(w_dw_a[l][:, None, :], (K_A, SUBLANES, D_A))
        nxt = (l + 1) % DEPTH
        h, acts, wg16, wu16, wd16, wo16, win16_next = _in_mix(
            xf, row(g_mix[l]), win16, b_in, wdwa8, row(b_dw_a[l]), row(ln_g_a[l]), row(ln_b_a[l]),
            w_pool_grp[l].astype(BF16), row(pool_scale[l]), w_dw_c[l],
            (w_gate, w_up, w_down, w_o, w_in), (l, l, l, l, nxt), (cb, tf, None, None, None), seq=s, tm=tm_mix)
        xf, h2 = _gate_merge(h, acts, xf, wg16, row(b_gate[l]), col_blocks(w_a_out[l]), col_blocks(w_b_out[l]),
                             col_blocks(w_c_out[l]), wo16, row(g_mlp[l]), tm=tm_merge, cb=cb)
        xf = _mlp(h2, xf, wu16, wd16, gf, final=(l == DEPTH - 1), tm=tm_mlp, tf=tf)
        win16 = win16_next
    return xf.reshape(b, s, d)
```

```python
import functools

import jax
import jax.numpy as jnp
from jax import lax
from jax.experimental import pallas as pl
from jax.experimental.pallas import tpu as pltpu

D_MODEL = 2048
DEPTH = 2
D_A = 768
K_A = 31
POOL_WINDOWS = (2, 4, 8, 16)
POOL_GROUP = 128
D_B = len(POOL_WINDOWS) * POOL_GROUP
D_C = 768
K_C = 3
D_IN = 2 * D_A + D_B + 3 * D_C
N_BRANCH = 3
D_FF = 4 * D_MODEL
EPS = 1e-6

OFF_AG = D_A
OFF_B = 2 * D_A
OFF_GB = OFF_B + D_B
OFF_GC = OFF_GB + D_C
OFF_XV = OFF_GC + D_C
D_ACTS = D_A + D_B + D_C
ACT_B = D_A
ACT_C = D_A + D_B

SUBLANES = 8
LANES = 128
HALO_A = 32
HALO_B = 16
HALO_C = 8
CONV_GROUPS = 4
Z_BLOCK = 256
Z_CHUNK = 2 * Z_BLOCK

V7X_VMEM_LIMIT = 60 * 1024 * 1024

BF16 = jnp.bfloat16
F32 = jnp.float32


def _rmsnorm(x, g):
    return x * lax.rsqrt(jnp.mean(x * x, axis=-1, keepdims=True) + EPS) * g


def _params(*sem, flags=None):
    return pltpu.CompilerParams(dimension_semantics=sem, vmem_limit_bytes=V7X_VMEM_LIMIT, flags=flags)


N_MIX_IN = 11
N_CAST = 5


def _in_mix_kernel(*refs, tm, tiles_per_seq):
    (x_ref, g_ref, win_ref, bin_ref, wdwa_ref, bdwa_ref, lng_ref, lnb_ref, wpool_ref, pscale_ref,
     wdwc_ref) = refs[:N_MIX_IN]
    h_ref, acts_ref = refs[N_MIX_IN + N_CAST:N_MIX_IN + N_CAST + 2]
    cast_refs = refs[N_MIX_IN:N_MIX_IN + N_CAST] + refs[N_MIX_IN + N_CAST + 2:N_MIX_IN + 2 * N_CAST + 2]
    gb_cur, ua_ext, cv_buf, zb_ext, p_ext = refs[N_MIX_IN + 2 * N_CAST + 2:]
    t = pl.program_id(0)
    s_prev = lax.rem(t, tiles_per_seq)

    @pl.when(s_prev == 0)
    def _():
        ua_ext[0:HALO_A, :] = jnp.zeros((HALO_A, D_A), F32)
        zb_ext[0:HALO_B, :] = jnp.zeros((HALO_B, D_B), F32)
        p_ext[0:HALO_C, :] = jnp.zeros((HALO_C, D_C), F32)

    h_ref[...] = _rmsnorm(x_ref[...], g_ref[...]).astype(BF16)

    cur_a, cur_b, cur_c = slice(HALO_A, HALO_A + tm), slice(HALO_B, HALO_B + tm), slice(HALO_C, HALO_C + tm)

    def put_block(lo, zc):
        if lo < OFF_AG:
            ua_ext[cur_a, lo:lo + Z_BLOCK] = zc
        elif lo < OFF_B:
            c = lo - OFF_AG
            ua_ext[cur_a, c:c + Z_BLOCK] = ua_ext[cur_a, c:c + Z_BLOCK] * jax.nn.sigmoid(zc)
        elif lo < OFF_GB:
            zb_ext[cur_b, lo - OFF_B:lo - OFF_B + Z_BLOCK] = zc
        elif lo < OFF_GC:
            gb_cur[:, lo - OFF_GB:lo - OFF_GB + Z_BLOCK] = zc
        elif lo < OFF_XV:
            p_ext[cur_c, lo - OFF_GC:lo - OFF_GC + Z_BLOCK] = zc
        else:
            c = lo - OFF_XV
            p_ext[cur_c, c:c + Z_BLOCK] = p_ext[cur_c, c:c + Z_BLOCK] * zc

    def project(lo):
        hi = min(lo + Z_CHUNK, D_IN)
        zc = jnp.dot(h_ref[...], win_ref[:, lo:hi], preferred_element_type=F32) + bin_ref[:, lo:hi]
        for b in range(0, hi - lo, Z_BLOCK):
            put_block(lo + b, zc[:, b:b + Z_BLOCK])
        return zc[0:SUBLANES, 0:LANES]

    rows = CONV_GROUPS * SUBLANES
    row_in_vreg = lax.broadcasted_iota(jnp.int32, (SUBLANES, LANES), 0)
    never = (row_in_vreg + t) < 0

    def conv_block(base, c, tok):
        lanes = slice(c * LANES, (c + 1) * LANES)
        lo = base
        n_win = HALO_A // SUBLANES + CONV_GROUPS
        win = [jnp.where(never, tok, ua_ext[lo + SUBLANES * j:lo + SUBLANES * (j + 1), lanes]) for j in range(n_win)]
        acc = [jnp.zeros((SUBLANES, LANES), F32) for _ in range(CONV_GROUPS)]
        for r in range(SUBLANES):
            if r == 0:
                moved = win
            else:
                rolled = [pltpu.roll(v, r, 0) for v in win]
                moved = [None] + [jnp.where(row_in_vreg >= r, rolled[j], rolled[j - 1]) for j in range(1, n_win)]
            for q in range(HALO_A // SUBLANES):
                lag = SUBLANES * q + r
                if lag >= K_A:
                    continue
                wt = wdwa_ref[K_A - 1 - lag, :, lanes]
                for gi in range(CONV_GROUPS):
                    acc[gi] = acc[gi] + moved[HALO_A // SUBLANES + gi - q] * wt
        cv_buf[base:base + rows, lanes] = jnp.concatenate(acc, axis=0)

    def norm_rows(base, tok):
        va = cv_buf[base:base + rows, :] + bdwa_ref[...]
        mu = jnp.mean(va, axis=-1, keepdims=True)
        xc = va - mu
        var = jnp.mean(xc * xc, axis=-1, keepdims=True)
        y = xc * lax.rsqrt(var + EPS) * lng_ref[...] + lnb_ref[...]
        y = y * jax.nn.sigmoid(y)
        acts_ref[base:base + rows, 0:D_A] = y.astype(BF16)

    def pool_group(gi, tok):
        w = POOL_WINDOWS[gi]
        lo = gi * POOL_GROUP
        seen = (s_prev * tm + 1 + lax.broadcasted_iota(jnp.int32, (tm, 1), 0)).astype(F32)
        tok = zb_ext[HALO_B:HALO_B + tm, lo:lo + POOL_GROUP]
        win = tok
        for lag in range(1, w):
            win = win + zb_ext[HALO_B - lag:HALO_B - lag + tm, lo:lo + POOL_GROUP]
        pooled = win / jnp.minimum(seen, float(w)) - tok
        mixed = jnp.dot(pooled.astype(BF16), wpool_ref[gi], preferred_element_type=F32)
        acts_ref[:, ACT_B + lo:ACT_B + lo + POOL_GROUP] = (mixed * pscale_ref[:, lo:lo + POOL_GROUP]).astype(BF16)

    def short_conv(tok):
        cv = p_ext[HALO_C:HALO_C + tm, :] * wdwc_ref[K_C - 1:K_C, :]
        for lag in range(1, K_C):
            tap = K_C - 1 - lag
            cv = cv + p_ext[HALO_C - lag:HALO_C - lag + tm, :] * wdwc_ref[tap:tap + 1, :]
        acts_ref[:, ACT_C:ACT_C + D_C] = (gb_cur[...] * cv).astype(BF16)

    def keep_tails(tok):
        ua_ext[0:HALO_A, :] = ua_ext[tm:tm + HALO_A, :]
        zb_ext[0:HALO_B, :] = zb_ext[tm:tm + HALO_B, :]
        p_ext[0:HALO_C, :] = p_ext[tm:tm + HALO_C, :]

    chunks = list(range(0, D_IN, Z_CHUNK))
    first = 2 * D_A // Z_CHUNK - 1
    early = (OFF_AG + Z_CHUNK) // Z_CHUNK * Z_CHUNK - OFF_AG
    early_work, conv_work = [], []
    for base in range(0, tm, rows):
        early_work += [functools.partial(conv_block, base, c) for c in range(early // LANES)]
        conv_work += [functools.partial(conv_block, base, c) for c in range(early // LANES, D_A // LANES)]
        conv_work += [functools.partial(norm_rows, base)]
    last = len(chunks) - 2
    per_chunk = -(-len(conv_work) // (last - first))
    for ci, lo in enumerate(chunks):
        tok = project(lo)
        if ci == first - 1:
            for work in early_work:
                work(tok)
        if first <= ci < last:
            for work in conv_work[(ci - first) * per_chunk:(ci - first + 1) * per_chunk]:
                work(tok)
        if ci == last - 1:
            for gi in range(len(POOL_WINDOWS)):
                pool_group(gi, tok)
    short_conv(tok)
    keep_tails(tok)

    for src_ref, dst_ref in zip(cast_refs[:N_CAST], cast_refs[N_CAST:]):
        if len(dst_ref.shape) == 2:
            dst_ref[...] = src_ref[...].astype(BF16)
        else:
            n_blk, _, width = dst_ref.shape
            for k in range(n_blk):
                dst_ref[k] = src_ref[:, k * width:(k + 1) * width].astype(BF16)


def _in_mix(x, g, win, b_in, wdwa8, bdwa, lng, lnb, wpool, pscale, wdwc, cast_stacks, cast_layers, cast_widths, *,
            seq, tm):
    m, d = x.shape
    n_t = m // tm
    cur = lambda t: (jnp.minimum(t, n_t - 1), 0)
    cast_in, cast_out, cast_shape = [], [], []
    for w, layer, width in zip(cast_stacks, cast_layers, cast_widths):
        _, r, c = w.shape
        assert r % n_t == 0
        cast_in.append(pl.BlockSpec((None, r // n_t, c), lambda t, layer=layer: (layer, jnp.minimum(t, n_t - 1), 0)))
        if width is None:
            cast_out.append(pl.BlockSpec((r // n_t, c), cur))
            cast_shape.append(jax.ShapeDtypeStruct((r, c), BF16))
        else:
            cast_out.append(pl.BlockSpec((c // width, r // n_t, width), lambda t: (0, jnp.minimum(t, n_t - 1), 0)))
            cast_shape.append(jax.ShapeDtypeStruct((c // width, r, width), BF16))
    const2 = lambda t: (0, 0)
    const3 = lambda t: (0, 0, 0)
    resident = functools.partial(pl.BlockSpec, pipeline_mode=pl.Buffered(1))
    return pl.pallas_call(
        functools.partial(_in_mix_kernel, tm=tm, tiles_per_seq=seq // tm),
        grid=(n_t,),
        in_specs=[
            pl.BlockSpec((tm, d), cur),
            resident((1, d), const2),
            resident((d, D_IN), const2),
            resident((1, D_IN), const2),
            resident((K_A, SUBLANES, D_A), const3),
            resident((1, D_A), const2),
            resident((1, D_A), const2),
            resident((1, D_A), const2),
            resident((len(POOL_WINDOWS), POOL_GROUP, POOL_GROUP), const3),
            resident((1, D_B), const2),
            resident((K_C, D_C), const2),
        ] + cast_in,
        out_specs=[pl.BlockSpec((tm, d), cur), pl.BlockSpec((tm, D_ACTS), cur)] + cast_out,
        out_shape=[jax.ShapeDtypeStruct((m, d), BF16), jax.ShapeDtypeStruct((m, D_ACTS), BF16)] + cast_shape,
        scratch_shapes=[
            pltpu.VMEM((tm, D_C), F32),
            pltpu.VMEM((HALO_A + tm, D_A), F32),
            pltpu.VMEM((tm, D_A), F32),
            pltpu.VMEM((HALO_B + tm, D_B), F32),
            pltpu.VMEM((HALO_C + tm, D_C), F32),
        ],
        compiler_params=_params("arbitrary"),
        name="in_mix",
    )(x, g, win, b_in, wdwa8, bdwa, lng, lnb, wpool, pscale, wdwc, *cast_stacks)


def _gate_merge_kernel(h_ref, acts_ref, x_ref, wg0_ref, wg1_ref, wg2_ref, bg0_ref, bg1_ref, bg2_ref,
                       wa_ref, wb_ref, wc_ref, wo_ref, gmlp_ref, o_ref, h2_ref):
    j = pl.program_id(1)
    cb = wo_ref.shape[0]

    def merged_cols(cols):
        def gate(wg_ref, bg_ref):
            return jax.nn.sigmoid(jnp.dot(h_ref[...], wg_ref[:, cols], preferred_element_type=F32) + bg_ref[:, cols])

        g0 = gate(wg0_ref, bg0_ref)
        g1 = gate(wg1_ref, bg1_ref)
        g2 = gate(wg2_ref, bg2_ref)
        mg = g0 * jnp.dot(acts_ref[:, 0:D_A], wa_ref[:, cols], preferred_element_type=F32)
        mg = mg + g1 * jnp.dot(acts_ref[:, ACT_B:ACT_B + D_B], wb_ref[:, cols], preferred_element_type=F32)
        mg = mg + g2 * jnp.dot(acts_ref[:, ACT_C:ACT_C + D_C], wc_ref[:, cols], preferred_element_type=F32)
        return mg.astype(BF16)

    @pl.when(j == 0)
    def _():
        o_ref[...] = x_ref[...]

    halves = [slice(0, cb // 2), slice(cb // 2, cb)]
    merged = [merged_cols(cols) for cols in halves]
    for cols, mg in zip(halves, merged):
        o_ref[...] += jnp.dot(mg, wo_ref[cols, :], preferred_element_type=F32)

    @pl.when(j == pl.num_programs(1) - 1)
    def _():
        h2_ref[...] = _rmsnorm(o_ref[...], gmlp_ref[...]).astype(BF16)


def _gate_merge(h, acts, x, wg, bg, wa, wb, wc, wo, gmlp, *, tm, cb):
    m, d = x.shape
    n_cb = d // cb
    tile = lambda i, j: (i, 0)
    blk = lambda i, j: (j, 0, 0)
    gate_blk = lambda k: (lambda i, j: (k * n_cb + j, 0, 0))
    gate_col = lambda k: (lambda i, j: (0, k * n_cb + j))
    assert wg.shape == (N_BRANCH * n_cb, d, cb) and wa.shape == (n_cb, D_A, cb)
    return pl.pallas_call(
        _gate_merge_kernel,
        grid=(m // tm, n_cb),
        in_specs=[
            pl.BlockSpec((tm, d), tile),
            pl.BlockSpec((tm, D_ACTS), tile),
            pl.BlockSpec((tm, d), tile),
            pl.BlockSpec((None, d, cb), gate_blk(0)),
            pl.BlockSpec((None, d, cb), gate_blk(1)),
            pl.BlockSpec((None, d, cb), gate_blk(2)),
            pl.BlockSpec((1, cb), gate_col(0)),
            pl.BlockSpec((1, cb), gate_col(1)),
            pl.BlockSpec((1, cb), gate_col(2)),
            pl.BlockSpec((None, D_A, cb), blk),
            pl.BlockSpec((None, D_B, cb), blk),
            pl.BlockSpec((None, D_C, cb), blk),
            pl.BlockSpec((cb, d), lambda i, j: (j, 0)),
            pl.BlockSpec((1, d), lambda i, j: (0, 0)),
        ],
        out_specs=[pl.BlockSpec((tm, d), tile), pl.BlockSpec((tm, d), tile)],
        out_shape=[jax.ShapeDtypeStruct((m, d), F32), jax.ShapeDtypeStruct((m, d), BF16)],
        compiler_params=_params("arbitrary", "arbitrary"),
        name="gate_merge",
    )(h, acts, x, wg, wg, wg, bg, bg, bg, wa, wb, wc, wo, gmlp)


def _mlp_kernel(h2_ref, x_ref, wu_ref, wd_ref, gf_ref, o_ref, *, final):
    f = pl.program_id(1)

    @pl.when(f == 0)
    def _():
        o_ref[...] = x_ref[...]

    a = jnp.dot(h2_ref[...], wu_ref[...], preferred_element_type=F32)
    a = jnp.square(jnp.maximum(a, 0.0)).astype(BF16)
    o_ref[...] += jnp.dot(a, wd_ref[...], preferred_element_type=F32)

    if final:
        @pl.when(f == pl.num_programs(1) - 1)
        def _():
            o_ref[...] = _rmsnorm(o_ref[...], gf_ref[...])


def _mlp(h2, x, wu, wd, gf, *, final, tm, tf):
    m, d = x.shape
    ff = wd.shape[0]
    assert wu.shape == (ff // tf, d, tf)
    tile = lambda i, f: (i, 0)
    return pl.pallas_call(
        functools.partial(_mlp_kernel, final=final),
        grid=(m // tm, ff // tf),
        in_specs=[
            pl.BlockSpec((tm, d), tile),
            pl.BlockSpec((tm, d), tile),
            pl.BlockSpec((None, d, tf), lambda i, f: (f, 0, 0)),
            pl.BlockSpec((tf, d), lambda i, f: (f, 0)),
            pl.BlockSpec((1, d), lambda i, f: (0, 0)),
        ],
        out_specs=pl.BlockSpec((tm, d), tile),
        out_shape=jax.ShapeDtypeStruct((m, d), F32),
        compiler_params=_params("arbitrary", "arbitrary"),
        name="mlp_final" if final else "mlp",
    )(h2, x, wu, wd, gf)


def kernel(x, g_mix, w_in, w_gate, b_gate, b_glu, w_dw_a, b_dw_a, ln_g_a, ln_b_a, w_a_out, w_pool_grp, pool_scale,
           w_b_out, w_dw_c, w_c_out, w_o, g_mlp, w_up, w_down, g_final):
    b, s, d = x.shape
    assert (d, w_in.shape[-1], w_up.shape[-1]) == (D_MODEL, D_IN, D_FF)
    xf = x.reshape(b * s, d)
    row = lambda v: v.reshape(1, -1)
    gf = row(g_final)
    win16 = w_in[0].astype(BF16)
    tm_mix, tm_merge, cb, tm_mlp, tf = 256, 512, 512, 1024, 512
    col_blocks = lambda w: w.astype(BF16).reshape(w.shape[0], -1, cb).transpose(1, 0, 2)
    for l in range(DEPTH):
        b_in = jnp.concatenate([b_glu[l], jnp.zeros((D_IN - 2 * D_A,), F32)]).reshape(1, D_IN)
        wdwa8 = jnp.broadcast_to(w_dw_a[l][:, None, :], (K_A, SUBLANES, D_A))
        nxt = (l + 1) % DEPTH
        h, acts, wg16, wu16, wd16, wo16, win16_next = _in_mix(
            xf, row(g_mix[l]), win16, b_in, wdwa8, row(b_dw_a[l]), row(ln_g_a[l]), row(ln_b_a[l]),
            w_pool_grp[l].astype(BF16), row(pool_scale[l]), w_dw_c[l],
            (w_gate, w_up, w_down, w_o, w_in), (l, l, l, l, nxt), (cb, tf, None, None, None), seq=s, tm=tm_mix)
        xf, h2 = _gate_merge(h, acts, xf, wg16, row(b_gate[l]), col_blocks(w_a_out[l]), col_blocks(w_b_out[l]),
                             col_blocks(w_c_out[l]), wo16, row(g_mlp[l]), tm=tm_merge, cb=cb)
        xf = _mlp(h2, xf, wu16, wd16, gf, final=(l == DEPTH - 1), tm=tm_mlp, tf=tf)
        win16 = win16_next
    return xf.reshape(b, s, d)
```
